```python
import math
import numpy as np
import jax
import jax.numpy as jnp
from jax import lax

D_MODEL = 1024
BATCH = 8
SEQ = 4096
DEPTH = 4

GRID_W = 64
CTX_LEN = 256

HEAD_DIM = 64
NA_HEADS = 4
RW_HEADS = 4
SSM_HEADS = 8
NA_W = NA_HEADS * HEAD_DIM
RW_W = RW_HEADS * HEAD_DIM
SSM_W = SSM_HEADS * HEAD_DIM
MIX_W = NA_W + RW_W + SSM_W

NA_WIN_R = 8
NA_WIN_C = 16

RW_DECAY_LORA = 32
RW_AAA_LORA = 32
RW_GATE_LORA = 64
RW_GN_EPS = 64e-5

SSM_GROUPS = 2
SSM_HPG = SSM_HEADS // SSM_GROUPS
SSM_STATE = 128
SSM_CONV = 5
SSM_CHUNK = 128
SSM_GN = SSM_GROUPS * SSM_STATE
SSM_XBC = SSM_W + 2 * SSM_GN

NA_COLS = 3 * NA_W
RW_COLS = 3 * RW_W + 2 * RW_DECAY_LORA + 2 * RW_AAA_LORA + RW_GATE_LORA
SSM_COLS = SSM_W + SSM_XBC + 2 * SSM_HEADS
IN_COLS = NA_COLS + RW_COLS + SSM_COLS

D_FF = 2816
N_EXPERTS = 8
TOP_K = 2
D_FF_EXPERT = 3584
MOE_BLOCK = 256
N_DENSE = (DEPTH + 1) // 2
N_MOE = DEPTH // 2

DEEPNORM_ALPHA = (2 * DEPTH) ** 0.25
DEEPNORM_BETA = (8 * DEPTH) ** -0.25
LN_EPS = 1e-5
RMS_EPS = 1e-5

kernel_name = 'hybrid_na_rwkv7_ssd_moe_diffusion_trunk'


def layer_norm(x, g, b):
    xf = x.astype(jnp.float32)
    mu = jnp.mean(xf, -1, keepdims=True)
    var = jnp.mean(jnp.square(xf - mu), -1, keepdims=True)
    return ((xf - mu) * lax.rsqrt(var + LN_EPS) * g + b).astype(x.dtype)


def swiglu(h, w_gate, w_up, w_down):
    return (jax.nn.silu(h @ w_gate) * (h @ w_up)) @ w_down


def centred_shift_mix(u, mu):
    up = jnp.pad(u, ((0, 0), (1, 1), (0, 0)))
    return u + mu * (0.5 * (up[:, :-2] + up[:, 2:]) - u)


def centred_dwconv(u, w, b):
    k = w.shape[0]
    y = lax.conv_general_dilated(u, w[:, None, :].astype(u.dtype), (1,), [(k // 2, k // 2)],
                                 dimension_numbers=('NWC', 'WIO', 'NWC'),
                                 feature_group_count=u.shape[-1])
    return y + b


def qkv_heads(u):
    bsz, t, _ = u.shape
    qkv = u.reshape(bsz, t, 3, NA_HEADS, HEAD_DIM)
    return qkv[:, :, 0], qkv[:, :, 1], qkv[:, :, 2]


def na_indices(rows):
    win_r = min(NA_WIN_R, rows)
    r = np.arange(rows)
    row_start = np.clip(r - win_r // 2, 0, rows - win_r)
    cidx = np.arange(GRID_W)
    col_start = np.clip(cidx - NA_WIN_C // 2, 0, GRID_W - NA_WIN_C)
    col_idx = col_start[:, None] + np.arange(NA_WIN_C)
    col_rel = col_idx - cidx[:, None] + NA_WIN_C - 1
    return win_r, row_start, col_idx, col_rel


def neighbourhood_attention(q, k, v, kc, vc, rpb):
    bsz, s, h, dh = q.shape
    rows = s // GRID_W
    win_r, row_start, col_idx, col_rel = na_indices(rows)
    n_loc = win_r * NA_WIN_C
    scale = dh ** -0.5
    qg = q.reshape(bsz, rows, GRID_W, h, dh)
    kg = k.reshape(bsz, rows, GRID_W, h, dh)
    vg = v.reshape(bsz, rows, GRID_W, h, dh)

    def one_row(args):
        r, r0 = args
        q_r = lax.dynamic_index_in_dim(qg, r, axis=1, keepdims=False)
        kb = lax.dynamic_slice_in_dim(kg, r0, win_r, axis=1)[:, :, col_idx]
        vb = lax.dynamic_slice_in_dim(vg, r0, win_r, axis=1)[:, :, col_idx]
        rel_r = r0 - r + NA_WIN_R - 1 + jnp.arange(win_r)
        bias = jnp.transpose(rpb[:, rel_r][:, :, col_rel], (0, 2, 1, 3))
        s_loc = jnp.einsum('bqhd,biqjhd->bhqij', q_r, kb).astype(jnp.float32) * scale + bias.astype(jnp.float32)
        s_ctx = jnp.einsum('bqhd,blhd->bhql', q_r, kc).astype(jnp.float32) * scale
        p = jax.nn.softmax(jnp.concatenate([s_loc.reshape(bsz, h, GRID_W, n_loc), s_ctx], -1), axis=-1)
        p = p.astype(v.dtype)
        p_loc = p[..., :n_loc].reshape(bsz, h, GRID_W, win_r, NA_WIN_C)
        return (jnp.einsum('bhqij,biqjhd->bqhd', p_loc, vb)
                + jnp.einsum('bhql,blhd->bqhd', p[..., n_loc:], vc))

    out = lax.map(one_row, (jnp.arange(rows, dtype=jnp.int32), jnp.asarray(row_start, dtype=jnp.int32)))
    return jnp.transpose(out, (1, 0, 2, 3, 4)).reshape(bsz, s, h * dh)


def context_attention(qc, kc, vc):
    bsz, l, h, dh = qc.shape
    s = jnp.einsum('bqhd,bkhd->bhqk', qc, kc).astype(jnp.float32) * dh ** -0.5
    p = jax.nn.softmax(s, axis=-1).astype(vc.dtype)
    return jnp.einsum('bhqk,bkhd->bqhd', p, vc).reshape(bsz, l, h * dh)


def rwkv_prepare(u, mu, w0, w2, a0, a2, g2, k_k, k_a):
    u = centred_shift_mix(u, mu).astype(jnp.float32)
    bsz, t, _ = u.shape
    s3 = 3 * RW_W
    s4 = s3 + 2 * RW_DECAY_LORA
    s5 = s4 + 2 * RW_AAA_LORA
    r, k, v = u[..., :RW_W], u[..., RW_W:2 * RW_W], u[..., 2 * RW_W:s3]
    wl = u[..., s3:s4].reshape(bsz, t, 2, RW_DECAY_LORA)
    al = u[..., s4:s5].reshape(bsz, t, 2, RW_AAA_LORA)
    gl = u[..., s5:]
    wlog = -jax.nn.softplus(-(w0 + jnp.einsum('btzr,zrc->btzc', jnp.tanh(wl), w2))) - 0.5
    decay = jnp.exp(-jnp.exp(wlog))
    aa = jax.nn.sigmoid(a0 + jnp.einsum('btzr,zrc->btzc', al, a2))
    g = jax.nn.sigmoid(gl) @ g2

    def heads(z):
        return z.reshape(*z.shape[:-1], RW_HEADS, HEAD_DIM)

    kk = heads(k * k_k)
    kk = kk * lax.rsqrt(jnp.sum(kk * kk, -1, keepdims=True) + 1e-12)
    kdir = k[:, :, None, :] * (1.0 + (aa - 1.0) * k_a)
    return heads(r), heads(decay), heads(kdir), heads(v), heads(aa), kk, g


def rwkv7_scan(r, w, k, v, a, b, state0, reverse):
    def step(S, inp):
        r_t, w_t, k_t, v_t, a_t, b_t = inp
        sa = jnp.einsum('bhvk,bhk->bhv', S, a_t)
        S = S * w_t[:, :, None, :] + sa[..., None] * b_t[:, :, None, :] + v_t[..., None] * k_t[:, :, None, :]
        return S, jnp.einsum('bhvk,bhk->bhv', S, r_t)

    xs = (jnp.moveaxis(r, 1, 0), jnp.moveaxis(w, 1, 0), jnp.moveaxis(k, 1, 0),
          jnp.moveaxis(v, 1, 0), jnp.moveaxis(a, 1, 0), jnp.moveaxis(b, 1, 0))
    S, ys = lax.scan(step, state0, xs, reverse=reverse)
    return jnp.moveaxis(ys, 0, 1), S


def rwkv_bidir(feat, s0_f, s0_b):
    r, decay, kdir, v, aa, kk, _ = feat
    y_f, s_f = rwkv7_scan(r, decay[:, :, 0], kdir[:, :, 0], v, -kk, kk * aa[:, :, 0], s0_f, False)
    y_b, s_b = rwkv7_scan(r, decay[:, :, 1], kdir[:, :, 1], v, -kk, kk * aa[:, :, 1], s0_b, True)
    return y_f + y_b, s_f, s_b


def rwkv_output(y, feat, r_k, ln_w, ln_b, out_dtype):
    r, _, kdir, v, _, _, g = feat
    bsz, t = y.shape[:2]
    mu = jnp.mean(y, -1, keepdims=True)
    var = jnp.mean(jnp.square(y - mu), -1, keepdims=True)
    yn = ((y - mu) * lax.rsqrt(var + RW_GN_EPS)).reshape(bsz, t, RW_W) * ln_w + ln_b
    bonus = jnp.sum(jnp.sum(r[:, :, None] * kdir * r_k, -1, keepdims=True), axis=2) * v
    return ((yn + bonus.reshape(bsz, t, RW_W)) * g).astype(out_dtype)


def ssd_chunked(xs, dt, a, bm, cm, init):
    bsz, t, g, e, p = xs.shape
    nc, q = t // SSM_CHUNK, SSM_CHUNK
    xdt = (xs * dt[..., None]).reshape(bsz, nc, q, g, e, p)
    a_cs = jnp.cumsum(jnp.moveaxis((dt * a).reshape(bsz, nc, q, g, e), 2, -1), axis=-1)
    bm = bm.reshape(bsz, nc, q, g, -1)
    cm = cm.reshape(bsz, nc, q, g, -1)
    tril = jnp.tril(jnp.ones((q, q), bool))
    seg = jnp.exp(jnp.where(tril, a_cs[..., :, None] - a_cs[..., None, :], -jnp.inf))
    cb = jnp.einsum('bclgn,bcsgn->bcgls', cm, bm)
    y_diag = jnp.einsum('bcgels,bcsgep->bclgep', cb[:, :, :, None] * seg, xdt)
    dec = jnp.moveaxis(jnp.exp(a_cs[..., -1:] - a_cs), -1, 2)
    states = jnp.einsum('bclgn,bclgep->bcgepn', bm, xdt * dec[..., None])
    states = jnp.concatenate([init[:, None], states], axis=1)
    chunk_cs = jnp.cumsum(jnp.pad(a_cs[..., -1], ((0, 0), (1, 0), (0, 0), (0, 0))), axis=1)
    tril_c = jnp.tril(jnp.ones((nc + 1, nc + 1), bool))[:, :, None, None]
    dchunk = jnp.exp(jnp.where(tril_c, chunk_cs[:, :, None] - chunk_cs[:, None, :], -jnp.inf))
    new_states = jnp.einsum('bzcge,bcgepn->bzgepn', dchunk, states)
    states_in, final = new_states[:, :-1], new_states[:, -1]
    y_off = jnp.einsum('bclgn,bcgepn->bclgep', cm, states_in) * jnp.moveaxis(jnp.exp(a_cs), -1, 2)[..., None]
    return (y_diag + y_off).reshape(bsz, t, g, e, p), final


def ssm_prepare(u, conv_w, conv_b, dt_bias):
    bsz, t, _ = u.shape
    z = u[..., :SSM_W]
    xbc = jax.nn.silu(centred_dwconv(u[..., SSM_W:SSM_W + SSM_XBC], conv_w, conv_b)).astype(jnp.float32)
    xs = xbc[..., :SSM_W].reshape(bsz, t, SSM_GROUPS, SSM_HPG, HEAD_DIM)
    bm = xbc[..., SSM_W:SSM_W + SSM_GN].reshape(bsz, t, SSM_GROUPS, SSM_STATE)
    cm = xbc[..., SSM_W + SSM_GN:].reshape(bsz, t, SSM_GROUPS, SSM_STATE)
    dt_raw = u[..., SSM_W + SSM_XBC:].astype(jnp.float32).reshape(bsz, t, 2, SSM_GROUPS, SSM_HPG)
    dt = jax.nn.softplus(dt_raw + dt_bias.reshape(2, SSM_GROUPS, SSM_HPG))
    return z, xs, bm, cm, dt


def ssm_bidir(feat, a_log, init_f, init_b):
    _, xs, bm, cm, dt = feat
    a = -jnp.exp(a_log.astype(jnp.float32)).reshape(2, SSM_GROUPS, SSM_HPG)
    y_f, s_f = ssd_chunked(xs, dt[:, :, 0], a[0], bm, cm, init_f)
    y_b, s_b = ssd_chunked(xs[:, ::-1], dt[:, ::-1, 1], a[1], bm[:, ::-1], cm[:, ::-1], init_b)
    return y_f + y_b[:, ::-1], s_f, s_b


def ssm_output(y, feat, d_skip, norm_w, out_dtype):
    z, xs, _, _, _ = feat
    bsz, t = y.shape[:2]
    y = y + d_skip.reshape(SSM_GROUPS, SSM_HPG)[..., None] * xs
    y = y.reshape(bsz, t, SSM_GROUPS, SSM_HPG * HEAD_DIM) * jax.nn.silu(z.astype(jnp.float32)).reshape(bsz, t, SSM_GROUPS, -1)
    y = y * lax.rsqrt(jnp.mean(jnp.square(y), -1, keepdims=True) + RMS_EPS)
    return (y.reshape(bsz, t, SSM_W) * norm_w).astype(out_dtype)


def token_mixers(u_lat, u_ctx, need_ctx, rpb, rw_mu, rw_w0, rw_w2, rw_a0, rw_a2, rw_g2,
                 rw_k_k, rw_k_a, rw_r_k, rw_ln_w, rw_ln_b,
                 conv_w, conv_b, dt_bias, a_log, d_skip, norm_w):
    out_dtype = u_lat.dtype
    bsz = u_lat.shape[0]
    s_a, s_b = NA_COLS, NA_COLS + RW_COLS
    q_l, k_l, v_l = qkv_heads(u_lat[..., :s_a])
    q_c, k_c, v_c = qkv_heads(u_ctx[..., :s_a])
    oa_lat = neighbourhood_attention(q_l, k_l, v_l, k_c, v_c, rpb)
    rw_args = (rw_mu, rw_w0, rw_w2, rw_a0, rw_a2, rw_g2, rw_k_k, rw_k_a)
    fb_ctx = rwkv_prepare(u_ctx[..., s_a:s_b], *rw_args)
    fb_lat = rwkv_prepare(u_lat[..., s_a:s_b], *rw_args)
    s0 = jnp.zeros((bsz, RW_HEADS, HEAD_DIM, HEAD_DIM), jnp.float32)
    yb_ctx, st_f, st_b = rwkv_bidir(fb_ctx, s0, s0)
    yb_lat, _, _ = rwkv_bidir(fb_lat, st_f, st_b)
    ob_lat = rwkv_output(yb_lat, fb_lat, rw_r_k, rw_ln_w, rw_ln_b, out_dtype)
    fc_ctx = ssm_prepare(u_ctx[..., s_b:], conv_w, conv_b, dt_bias)
    fc_lat = ssm_prepare(u_lat[..., s_b:], conv_w, conv_b, dt_bias)
    h0 = jnp.zeros((bsz, SSM_GROUPS, SSM_HPG, HEAD_DIM, SSM_STATE), jnp.float32)
    yc_ctx, hs_f, hs_b = ssm_bidir(fc_ctx, a_log, h0, h0)
    yc_lat, _, _ = ssm_bidir(fc_lat, a_log, hs_f, hs_b)
    oc_lat = ssm_output(yc_lat, fc_lat, d_skip, norm_w, out_dtype)
    o_lat = jnp.concatenate([oa_lat, ob_lat, oc_lat], axis=-1)
    if not need_ctx:
        return o_lat, None
    o_ctx = jnp.concatenate([context_attention(q_c, k_c, v_c),
                             rwkv_output(yb_ctx, fb_ctx, rw_r_k, rw_ln_w, rw_ln_b, out_dtype),
                             ssm_output(yc_ctx, fc_ctx, d_skip, norm_w, out_dtype)], axis=-1)
    return o_lat, o_ctx


def moe_swiglu(h, router_w, router_b, w_gate, w_up, w_down):
    n_tok, d = h.shape
    logits = (h @ router_w).astype(jnp.float32) + router_b
    top_val, top_idx = lax.top_k(logits, TOP_K)
    gates = jax.nn.softmax(top_val, axis=-1)
    n_assign = n_tok * TOP_K
    flat_e = top_idx.reshape(-1)
    flat_tok = jnp.repeat(jnp.arange(n_tok, dtype=jnp.int32), TOP_K)
    order = jnp.argsort(flat_e)
    e_sorted, tok_sorted, g_sorted = flat_e[order], flat_tok[order], gates.reshape(-1)[order]
    counts = jnp.bincount(flat_e, length=N_EXPERTS)
    padded = (counts + MOE_BLOCK - 1) // MOE_BLOCK * MOE_BLOCK
    pad_end = jnp.cumsum(padded)
    pad_start = pad_end - padded
    start = jnp.cumsum(counts) - counts
    dest = pad_start[e_sorted] + jnp.arange(n_assign) - start[e_sorted]
    n_blocks = -(-n_assign // MOE_BLOCK) + N_EXPERTS
    slot_tok = jnp.full((n_blocks * MOE_BLOCK,), n_tok, jnp.int32).at[dest].set(tok_sorted)
    block_e = jnp.minimum(jnp.searchsorted(pad_end, jnp.arange(n_blocks) * MOE_BLOCK, side='right'), N_EXPERTS - 1)
    h_pad = jnp.concatenate([h, jnp.zeros((1, d), h.dtype)], axis=0)
    xb = h_pad[slot_tok].reshape(n_blocks, MOE_BLOCK, d)

    def expert_block(args):
        xblk, e = args
        return (jax.nn.silu(xblk @ w_gate[e]) * (xblk @ w_up[e])) @ w_down[e]

    yb = lax.map(expert_block, (xb, block_e)).reshape(-1, d)
    y_sorted = yb[dest] * g_sorted[:, None].astype(yb.dtype)
    return jnp.zeros_like(h).at[tok_sorted].add(y_sorted)


def setup_inputs(seed: int = 0) -> dict:
    key = jax.random.key(seed)
    k = jax.random.split(key, 38)
    f32 = jnp.float32

    def nrm(i, shape, s):
        return jax.random.normal(k[i], shape, f32) * s

    def uni(i, shape, lo, hi):
        return jax.random.uniform(k[i], shape, f32, lo, hi)

    L = DEPTH
    dt0 = jnp.exp(uni(26, (L, 2, SSM_HEADS), math.log(1e-3), math.log(1e-1)))
    return {
        'x': nrm(0, (BATCH, SEQ, D_MODEL), 1.0),
        'c': nrm(1, (BATCH, D_MODEL), 1.0),
        'ctx': nrm(2, (BATCH, CTX_LEN, D_MODEL), 1.0),
        'c_ctx': nrm(3, (D_MODEL,), 1.0),
        'w_mod': nrm(4, (L, D_MODEL, 6 * D_MODEL), 0.5 * D_MODEL ** -0.5),
        'b_mod': nrm(5, (L, 6 * D_MODEL), 0.02),
        'w_in': nrm(6, (L, D_MODEL, IN_COLS), D_MODEL ** -0.5),
        'w_out': nrm(7, (L, MIX_W, D_MODEL), DEEPNORM_BETA * MIX_W ** -0.5),
        'ln1_g': 1.0 + nrm(8, (L, D_MODEL), 0.1),
        'ln1_b': nrm(9, (L, D_MODEL), 0.02),
        'ln2_g': 1.0 + nrm(10, (L, D_MODEL), 0.1),
        'ln2_b': nrm(11, (L, D_MODEL), 0.02),
        'na_rpb': nrm(12, (L, NA_HEADS, 2 * NA_WIN_R - 1, 2 * NA_WIN_C - 1), 0.1),
        'rw_mu': uni(13, (L, RW_COLS), 0.0, 1.0),
        'rw_w0': uni(14, (L, 2, RW_W), -6.0, 0.0),
        'rw_w2': nrm(15, (L, 2, RW_DECAY_LORA, RW_W), 0.5 * RW_DECAY_LORA ** -0.5),
        'rw_a0': nrm(16, (L, 2, RW_W), 0.5),
        'rw_a2': nrm(17, (L, 2, RW_AAA_LORA, RW_W), 0.5 * RW_AAA_LORA ** -0.5),
        'rw_g2': nrm(18, (L, RW_GATE_LORA, RW_W), RW_GATE_LORA ** -0.5),
        'rw_k_k': 0.85 + nrm(19, (L, RW_W), 0.05),
        'rw_k_a': 1.0 + nrm(20, (L, RW_W), 0.05),
        'rw_r_k': nrm(21, (L, RW_HEADS, HEAD_DIM), 0.1),
        'rw_ln_w': 1.0 + nrm(22, (L, RW_W), 0.1),
        'rw_ln_b': nrm(23, (L, RW_W), 0.02),
        'ssm_conv_w': nrm(24, (L, SSM_CONV, SSM_XBC), SSM_CONV ** -0.5),
        'ssm_conv_b': nrm(25, (L, SSM_XBC), 0.02),
        'ssm_dt_bias': dt0 + jnp.log(-jnp.expm1(-dt0)),
        'ssm_a_log': jnp.log(uni(27, (L, 2, SSM_HEADS), 1.0, 16.0)),
        'ssm_d': 1.0 + nrm(28, (L, SSM_HEADS), 0.1),
        'ssm_norm_w': 1.0 + nrm(29, (L, SSM_W), 0.1),
        'ffn_w_gate': nrm(30, (N_DENSE, D_MODEL, D_FF), D_MODEL ** -0.5),
        'ffn_w_up': nrm(31, (N_DENSE, D_MODEL, D_FF), D_MODEL ** -0.5),
        'ffn_w_down': nrm(32, (N_DENSE, D_FF, D_MODEL), DEEPNORM_BETA * D_FF ** -0.5),
        'moe_router': nrm(33, (N_MOE, D_MODEL, N_EXPERTS), D_MODEL ** -0.5),
        'moe_router_b': nrm(34, (N_MOE, N_EXPERTS), 0.01),
        'moe_w_gate': nrm(35, (N_MOE, N_EXPERTS, D_MODEL, D_FF_EXPERT), D_MODEL ** -0.5),
        'moe_w_up': nrm(36, (N_MOE, N_EXPERTS, D_MODEL, D_FF_EXPERT), D_MODEL ** -0.5),
        'moe_w_down': nrm(37, (N_MOE, N_EXPERTS, D_FF_EXPERT, D_MODEL), DEEPNORM_BETA * D_FF_EXPERT ** -0.5),
    }


def reference(x, c, ctx, c_ctx, w_mod, b_mod, w_in, w_out, ln1_g, ln1_b, ln2_g, ln2_b,
              na_rpb, rw_mu, rw_w0, rw_w2, rw_a0, rw_a2, rw_g2, rw_k_k, rw_k_a, rw_r_k, rw_ln_w, rw_ln_b,
              ssm_conv_w, ssm_conv_b, ssm_dt_bias, ssm_a_log, ssm_d, ssm_norm_w,
              ffn_w_gate, ffn_w_up, ffn_w_down,
              moe_router, moe_router_b, moe_w_gate, moe_w_up, moe_w_down):
    bsz, seq, d = x.shape
    ctx_len = ctx.shape[1]
    h_lat, h_ctx = x, ctx
    for l in range(DEPTH):
        need_ctx = l < DEPTH - 1
        mod_lat = (jax.nn.silu(c) @ w_mod[l] + b_mod[l])[:, None, :]
        mod_ctx = jax.nn.silu(c_ctx) @ w_mod[l] + b_mod[l]
        sh_a, sc_a, g_a, sh_f, sc_f, g_f = jnp.split(mod_lat, 6, axis=-1)
        csh_a, csc_a, cg_a, csh_f, csc_f, cg_f = jnp.split(mod_ctx, 6, axis=-1)

        u_lat = (h_lat * (1.0 + sc_a) + sh_a) @ w_in[l]
        u_ctx = (h_ctx * (1.0 + csc_a) + csh_a) @ w_in[l]
        o_lat, o_ctx = token_mixers(u_lat, u_ctx, need_ctx, na_rpb[l], rw_mu[l], rw_w0[l], rw_w2[l],
                                    rw_a0[l], rw_a2[l], rw_g2[l], rw_k_k[l], rw_k_a[l], rw_r_k[l],
                                    rw_ln_w[l], rw_ln_b[l], ssm_conv_w[l], ssm_conv_b[l],
                                    ssm_dt_bias[l], ssm_a_log[l], ssm_d[l], ssm_norm_w[l])
        h_lat = layer_norm(DEEPNORM_ALPHA * h_lat + g_a * (o_lat @ w_out[l]), ln1_g[l], ln1_b[l])
        if need_ctx:
            h_ctx = layer_norm(DEEPNORM_ALPHA * h_ctx + cg_a * (o_ctx @ w_out[l]), ln1_g[l], ln1_b[l])

        j = l // 2
        f_lat = h_lat * (1.0 + sc_f) + sh_f
        f_ctx = h_ctx * (1.0 + csc_f) + csh_f if need_ctx else None
        if l % 2 == 0:
            y_lat = swiglu(f_lat, ffn_w_gate[j], ffn_w_up[j], ffn_w_down[j])
            y_ctx = swiglu(f_ctx, ffn_w_gate[j], ffn_w_up[j], ffn_w_down[j]) if need_ctx else None
        else:
            toks = f_lat.reshape(-1, d)
            if need_ctx:
                toks = jnp.concatenate([toks, f_ctx.reshape(-1, d)], axis=0)
            y = moe_swiglu(toks, moe_router[j], moe_router_b[j], moe_w_gate[j], moe_w_up[j], moe_w_down[j])
            y_lat = y[:bsz * seq].reshape(bsz, seq, d)
            y_ctx = y[bsz * seq:].reshape(bsz, ctx_len, d) if need_ctx else None
        h_lat = layer_norm(DEEPNORM_ALPHA * h_lat + g_f * y_lat, ln2_g[l], ln2_b[l])
        if need_ctx:
            h_ctx = layer_norm(DEEPNORM_ALPHA * h_ctx + cg_f * y_ctx, ln2_g[l], ln2_b[l])
    return h_lat
```

```python
import functools

import numpy as np
import jax
import jax.numpy as jnp
from jax import lax
from jax.experimental import pallas as pl
from jax.experimental.pallas import tpu as pltpu

F32 = jnp.float32
BF16 = jnp.bfloat16

D_MODEL = 1024
DEPTH = 4
GRID_W = 64
HEAD_DIM = 64
NA_HEADS = 4
RW_HEADS = 4
SSM_HEADS = 8
NA_W = NA_HEADS * HEAD_DIM
RW_W = RW_HEADS * HEAD_DIM
SSM_W = SSM_HEADS * HEAD_DIM
NA_WIN_R = 8
NA_WIN_C = 16
RW_DECAY_LORA = 32
RW_AAA_LORA = 32
RW_GATE_LORA = 64
RW_GN_EPS = 64e-5
SSM_GROUPS = 2
SSM_HPG = SSM_HEADS // SSM_GROUPS
SSM_STATE = 128
SSM_CONV = 5
SSM_CHUNK = 128
SSM_GN = SSM_GROUPS * SSM_STATE
SSM_XBC = SSM_W + 2 * SSM_GN
NA_COLS = 3 * NA_W
RW_COLS = 3 * RW_W + 2 * RW_DECAY_LORA + 2 * RW_AAA_LORA + RW_GATE_LORA
SSM_COLS = SSM_W + SSM_XBC + 2 * SSM_HEADS
N_EXPERTS = 8
TOP_K = 2
MOE_BLOCK = 256
DEEPNORM_ALPHA = (2 * DEPTH) ** 0.25
LN_EPS = 1e-5
RMS_EPS = 1e-5

LANES = 128
TM = 256
RW_PAD = 1024
SSM_PAD = 1664
ROUTER_PAD = LANES
VMEM_LIMIT = 56 * 1024 * 1024
MASK_NEG = -1e30


def _cparams(*sem):
    return pltpu.CompilerParams(dimension_semantics=sem, vmem_limit_bytes=VMEM_LIMIT)


def _dot(a, b):
    return jnp.dot(a, b, preferred_element_type=F32)


def _dot_nt(a, b):
    return lax.dot_general(a, b, (((1,), (1,)), ((), ())), preferred_element_type=F32)


def _layer_norm(z, g, b):
    mu = jnp.mean(z, -1, keepdims=True)
    zc = z - mu
    var = jnp.mean(zc * zc, -1, keepdims=True)
    return zc * lax.rsqrt(var + LN_EPS) * g + b


def _mod_kernel(c_ref, w_ref, b_ref, o_ref):
    cs = c_ref[...]
    cs = cs * jax.nn.sigmoid(cs)
    o_ref[0] = _dot(cs.astype(BF16), w_ref[0].astype(BF16)) + b_ref[0]


def modulation(cin, w_mod, b_mod):
    r, d = cin.shape
    nl, _, n = w_mod.shape
    tn = d
    return pl.pallas_call(
        _mod_kernel,
        grid=(nl, n // tn),
        in_specs=[pl.BlockSpec((r, d), lambda l, j: (0, 0)),
                  pl.BlockSpec((1, d, tn), lambda l, j: (l, 0, j)),
                  pl.BlockSpec((1, 1, tn), lambda l, j: (l, 0, j))],
        out_specs=pl.BlockSpec((1, r, tn), lambda l, j: (l, 0, j)),
        out_shape=jax.ShapeDtypeStruct((nl, r, n), F32),
        compiler_params=_cparams("parallel", "parallel"),
        name="modulation",
    )(cin, w_mod, b_mod.reshape(nl, 1, n))


def _mod_specs(d, idx_a, idx_b):
    return [pl.BlockSpec((1, 1, d), lambda i: (i, 0, idx_a)),
            pl.BlockSpec((1, 1, d), lambda i: (i, 0, idx_b))]


def _adaln_proj_kernel(x_ref, sh_ref, sc_ref, w_ref, *o_refs, splits):
    xm = x_ref[...] * (1.0 + sc_ref[0]) + sh_ref[0]
    y = _dot(xm.astype(BF16), w_ref[...])
    off = 0
    for o_ref, n in zip(o_refs, splits):
        o_ref[...] = y[:, off:off + n].astype(o_ref.dtype)
        off += n


def adaln_proj(x, mod_tab, sh_idx, sc_idx, w, splits, dtypes):
    nt, d = x.shape
    n = w.shape[1]
    return pl.pallas_call(
        functools.partial(_adaln_proj_kernel, splits=splits),
        grid=(nt // TM,),
        in_specs=[pl.BlockSpec((TM, d), lambda i: (i, 0))] + _mod_specs(d, sh_idx, sc_idx)
        + [pl.BlockSpec((d, n), lambda i: (0, 0))],
        out_specs=[pl.BlockSpec((TM, s), lambda i: (i, 0)) for s in splits],
        out_shape=[jax.ShapeDtypeStruct((nt, s), dt) for s, dt in zip(splits, dtypes)],
        compiler_params=_cparams("parallel"),
        name="adaln_proj",
    )(x, mod_tab, mod_tab, w)


def _adaln_swiglu_kernel(x_ref, sh_ref, sc_ref, wg_ref, wu_ref, o_ref):
    xm = (x_ref[...] * (1.0 + sc_ref[0]) + sh_ref[0]).astype(BF16)
    g = _dot(xm, wg_ref[...])
    u = _dot(xm, wu_ref[...])
    o_ref[...] = (g * jax.nn.sigmoid(g) * u).astype(o_ref.dtype)


def adaln_swiglu(x, mod_tab, sh_idx, sc_idx, wg, wu):
    nt, d = x.shape
    f = wg.shape[1]
    return pl.pallas_call(
        _adaln_swiglu_kernel,
        grid=(nt // TM,),
        in_specs=[pl.BlockSpec((TM, d), lambda i: (i, 0))] + _mod_specs(d, sh_idx, sc_idx)
        + [pl.BlockSpec((d, f), lambda i: (0, 0)), pl.BlockSpec((d, f), lambda i: (0, 0))],
        out_specs=pl.BlockSpec((TM, f), lambda i: (i, 0)),
        out_shape=jax.ShapeDtypeStruct((nt, f), BF16),
        compiler_params=_cparams("parallel"),
        name="adaln_swiglu",
    )(x, mod_tab, mod_tab, wg, wu)


def _proj_res_ln_kernel(o_ref, w_ref, h_ref, gate_ref, g_ref, b_ref, out_ref):
    y = _dot(o_ref[...], w_ref[...])
    z = DEEPNORM_ALPHA * h_ref[...] + gate_ref[0] * y
    out_ref[...] = _layer_norm(z, g_ref[...], b_ref[...])


def proj_res_ln(o, w, h, mod_tab, gate_idx, ln_g, ln_b):
    nt, k = o.shape
    d = w.shape[1]
    return pl.pallas_call(
        _proj_res_ln_kernel,
        grid=(nt // TM,),
        in_specs=[pl.BlockSpec((TM, k), lambda i: (i, 0)),
                  pl.BlockSpec((k, d), lambda i: (0, 0)),
                  pl.BlockSpec((TM, d), lambda i: (i, 0)),
                  pl.BlockSpec((1, 1, d), lambda i: (i, 0, gate_idx)),
                  pl.BlockSpec((1, d), lambda i: (0, 0)),
                  pl.BlockSpec((1, d), lambda i: (0, 0))],
        out_specs=pl.BlockSpec((TM, d), lambda i: (i, 0)),
        out_shape=jax.ShapeDtypeStruct((nt, d), F32),
        compiler_params=_cparams("parallel"),
        name="proj_res_ln",
    )(o, w, h, mod_tab, ln_g.reshape(1, d), ln_b.reshape(1, d))


def _na_bias_table(rpb, rows):
    win_r = min(NA_WIN_R, rows)
    q = np.arange(GRID_W)
    col_start = np.clip(q - NA_WIN_C // 2, 0, GRID_W - NA_WIN_C)
    kc = np.arange(GRID_W)
    valid = (kc[None, :] >= col_start[:, None]) & (kc[None, :] < col_start[:, None] + NA_WIN_C)
    crel = np.clip(kc[None, :] - q[:, None] + NA_WIN_C - 1, 0, 2 * NA_WIN_C - 2)
    case = np.arange(NA_WIN_R)
    rrel = np.clip(NA_WIN_R - 1 - case[:, None] + np.arange(win_r)[None, :], 0, 2 * NA_WIN_R - 2)
    t = rpb[:, rrel]
    t = t[:, :, :, crel]
    t = jnp.transpose(t, (1, 0, 3, 2, 4))
    t = jnp.where(valid[None, None, :, None, :], t, MASK_NEG)
    return t.reshape(NA_WIN_R, NA_HEADS, GRID_W, win_r * GRID_W).astype(F32)


def _attn_heads(q, k_list, v_list, bias_list, scale):
    outs = []
    for h in range(NA_HEADS):
        sl = slice(h * HEAD_DIM, (h + 1) * HEAD_DIM)
        qh = q[:, sl]
        scores = []
        for kk, bias in zip(k_list, bias_list):
            s = _dot_nt(qh, kk[:, sl]) * scale
            if bias is not None:
                s = s + bias[0, h]
            scores.append(s)
        m = scores[0].max(-1, keepdims=True)
        for s in scores[1:]:
            m = jnp.maximum(m, s.max(-1, keepdims=True))
        den = 0.0
        acc = 0.0
        for s, vv in zip(scores, v_list):
            p = jnp.exp(s - m)
            den = den + p.sum(-1, keepdims=True)
            acc = acc + _dot(p.astype(BF16), vv[:, sl])
        outs.append(acc / den)
    return jnp.concatenate(outs, axis=-1)


def _na_kernel(u_ref, bias_ref, o_ref, *, ctx_len, rows, scale):
    j = pl.program_id(1)
    n_ctx_blk = ctx_len // GRID_W
    win_r = min(NA_WIN_R, rows)
    kc = u_ref[0, 0:ctx_len, NA_W:2 * NA_W]
    vc = u_ref[0, 0:ctx_len, 2 * NA_W:3 * NA_W]

    @pl.when(j < n_ctx_blk)
    def _ctx():
        q = u_ref[0, pl.ds(pl.multiple_of(j * GRID_W, GRID_W), GRID_W), 0:NA_W]
        o_ref[0] = _attn_heads(q, [kc], [vc], [None], scale).astype(o_ref.dtype)

    @pl.when(j >= n_ctx_blk)
    def _lat():
        r = j - n_ctx_blk
        r0 = jnp.clip(r - win_r // 2, 0, rows - win_r)
        q = u_ref[0, pl.ds(pl.multiple_of(ctx_len + r * GRID_W, GRID_W), GRID_W), 0:NA_W]
        kstart = pl.multiple_of(ctx_len + r0 * GRID_W, GRID_W)
        kl = u_ref[0, pl.ds(kstart, win_r * GRID_W), NA_W:2 * NA_W]
        vl = u_ref[0, pl.ds(kstart, win_r * GRID_W), 2 * NA_W:3 * NA_W]
        o_ref[0] = _attn_heads(q, [kl, kc], [vl, vc], [bias_ref, None], scale).astype(o_ref.dtype)


def na_attention(ua, bias_tab, ctx_len):
    bsz, tp, _ = ua.shape
    rows = (tp - ctx_len) // GRID_W
    win_r = min(NA_WIN_R, rows)
    n_ctx_blk = ctx_len // GRID_W

    def bias_idx(b, j):
        r = jnp.clip(j - n_ctx_blk, 0, rows - 1)
        r0 = jnp.clip(r - win_r // 2, 0, rows - win_r)
        return (r - r0, 0, 0, 0)

    return pl.pallas_call(
        functools.partial(_na_kernel, ctx_len=ctx_len, rows=rows, scale=HEAD_DIM ** -0.5),
        grid=(bsz, tp // GRID_W),
        in_specs=[pl.BlockSpec((1, tp, NA_COLS), lambda b, j: (b, 0, 0)),
                  pl.BlockSpec((1, NA_HEADS, GRID_W, win_r * GRID_W), bias_idx)],
        out_specs=pl.BlockSpec((1, GRID_W, NA_W), lambda b, j: (b, j, 0)),
        out_shape=jax.ShapeDtypeStruct((bsz, tp, NA_W), BF16),
        compiler_params=_cparams("parallel", "arbitrary"),
        name="na_attention",
    )(ua, bias_tab)


RW_KP = HEAD_DIM // 2
RW_TBLK = 64


def _rwkv_scan_kernel(a_ref, w_ref, b_ref, k_ref, r_ref, v_ref, y_ref, s_ref, *, tblk):
    @pl.when(pl.program_id(0) == 0)
    def _init():
        s_ref[...] = jnp.zeros_like(s_ref)

    half = LANES // 2

    def step(t, carry):
        a_t = a_ref[t]
        w_t = w_ref[t]
        b_t = b_ref[t]
        k_t = k_ref[t]
        r_t = r_ref[t]
        v_t = v_ref[t]
        sa = jnp.zeros((HEAD_DIM, LANES), F32)
        for kp in range(RW_KP):
            sa = sa + s_ref[kp] * a_t[kp:kp + 1, :]
        sa = sa + pltpu.roll(sa, half, 1)
        y = jnp.zeros((HEAD_DIM, LANES), F32)
        for kp in range(RW_KP):
            s_new = s_ref[kp] * w_t[kp:kp + 1, :] + sa * b_t[kp:kp + 1, :] + v_t * k_t[kp:kp + 1, :]
            s_ref[kp] = s_new
            y = y + s_new * r_t[kp:kp + 1, :]
        y_ref[t] = y + pltpu.roll(y, half, 1)
        return carry

    lax.fori_loop(0, tblk, step, 0)


def rwkv_scan(a, w, b, k, r, v, tblk=RW_TBLK):
    t = a.shape[0]
    kspec = pl.BlockSpec((tblk, RW_KP, LANES), lambda i: (i, 0, 0))
    vspec = pl.BlockSpec((tblk, HEAD_DIM, LANES), lambda i: (i, 0, 0))
    return pl.pallas_call(
        functools.partial(_rwkv_scan_kernel, tblk=tblk),
        grid=(t // tblk,),
        in_specs=[kspec] * 5 + [vspec],
        out_specs=vspec,
        out_shape=jax.ShapeDtypeStruct((t, HEAD_DIM, LANES), F32),
        scratch_shapes=[pltpu.VMEM((RW_KP, HEAD_DIM, LANES), F32)],
        compiler_params=_cparams("arbitrary"),
        name="rwkv_scan",
    )(a, w, b, k, r, v)


def _rw_key_layout(zf, zb):
    z = jnp.stack([zf, zb], 0)
    _, bsz, t, nh, n = z.shape
    z = z.reshape(2, bsz, t, nh, 2, n // 2)
    z = jnp.transpose(z, (2, 5, 4, 0, 1, 3))
    return z.reshape(t, n // 2, 2 * 2 * bsz * nh)


def _rw_val_layout(zf, zb):
    z = jnp.stack([zf, zb], 0)
    _, bsz, t, nh, n = z.shape
    z = jnp.transpose(z, (2, 4, 0, 1, 3)).reshape(t, n, 2 * bsz * nh)
    return jnp.concatenate([z, z], axis=-1)


def _shift_mix(u, mu):
    up = jnp.pad(u, ((0, 0), (1, 1), (0, 0)))
    return u + mu * (0.5 * (up[:, :-2] + up[:, 2:]) - u)


def rwkv_mixer(u, ctx_len, mu, w0, w2, a0, a2, g2, k_k, k_a, r_k, ln_w, ln_b):
    bsz, t, _ = u.shape
    hp = lax.Precision.HIGHEST
    u = jnp.concatenate([_shift_mix(u[:, :ctx_len], mu), _shift_mix(u[:, ctx_len:], mu)], axis=1)
    s3 = 3 * RW_W
    s4 = s3 + 2 * RW_DECAY_LORA
    s5 = s4 + 2 * RW_AAA_LORA
    r, k, v = u[..., :RW_W], u[..., RW_W:2 * RW_W], u[..., 2 * RW_W:s3]
    wl = u[..., s3:s4].reshape(bsz, t, 2, RW_DECAY_LORA)
    al = u[..., s4:s5].reshape(bsz, t, 2, RW_AAA_LORA)
    gl = u[..., s5:]
    wlog = -jax.nn.softplus(-(w0 + jnp.einsum('btzr,zrc->btzc', jnp.tanh(wl), w2, precision=hp))) - 0.5
    decay = jnp.exp(-jnp.exp(wlog))
    aa = jax.nn.sigmoid(a0 + jnp.einsum('btzr,zrc->btzc', al, a2, precision=hp))
    g = jnp.dot(jax.nn.sigmoid(gl), g2, precision=hp)

    def heads(z):
        return z.reshape(*z.shape[:-1], RW_HEADS, HEAD_DIM)

    kk = heads(k * k_k)
    kk = kk * lax.rsqrt(jnp.sum(kk * kk, -1, keepdims=True) + 1e-12)
    kdir = heads(k[:, :, None, :] * (1.0 + (aa - 1.0) * k_a))
    decay, aa = heads(decay), heads(aa)
    rh, vh = heads(r), heads(v)

    def rev(z):
        return jnp.concatenate([jnp.flip(z[:, :ctx_len], 1), jnp.flip(z[:, ctx_len:], 1)], axis=1)

    y = rwkv_scan(
        _rw_key_layout(-kk, rev(-kk)),
        _rw_key_layout(decay[:, :, 0], rev(decay[:, :, 1])),
        _rw_key_layout(kk * aa[:, :, 0], rev(kk * aa[:, :, 1])),
        _rw_key_layout(kdir[:, :, 0], rev(kdir[:, :, 1])),
        _rw_key_layout(rh, rev(rh)),
        _rw_val_layout(vh, rev(vh)))
    y = y[:, :, :LANES // 2].reshape(t, HEAD_DIM, 2, bsz, RW_HEADS)
    y = jnp.transpose(y, (2, 3, 0, 4, 1))
    y = y[0] + rev(y[1])

    mean = jnp.mean(y, -1, keepdims=True)
    var = jnp.mean(jnp.square(y - mean), -1, keepdims=True)
    yn = ((y - mean) * lax.rsqrt(var + RW_GN_EPS)).reshape(bsz, t, RW_W) * ln_w + ln_b
    bonus = jnp.sum(jnp.sum(rh[:, :, None] * kdir * r_k, -1, keepdims=True), axis=2) * vh
    return ((yn + bonus.reshape(bsz, t, RW_W)) * g).astype(BF16)


def _ssd_kernel(x_ref, bt_ref, c_ref, b_ref, dt_ref, dtt_ref, a_ref, at_ref, y_ref, st_ref):
    d = pl.program_id(1)
    j = pl.program_id(2)
    q = SSM_CHUNK
    hp = lax.Precision.HIGHEST

    @pl.when(j == 0)
    def _init():
        st_ref[...] = jnp.zeros_like(st_ref)

    row = lax.broadcasted_iota(jnp.int32, (q, q), 0)
    col = lax.broadcasted_iota(jnp.int32, (q, q), 1)
    mask = (row - col) * (1 - 2 * d) >= 0
    tri = mask.astype(F32)
    dt = dt_ref[0, 0]
    dta = dt * a_ref[0]
    dta_t = dtt_ref[0, 0] * at_ref[0]
    acs = jnp.dot(tri, dta, precision=hp, preferred_element_type=F32)
    acs_t = lax.dot_general(dta_t, tri, (((1,), (1,)), ((), ())), precision=hp,
                            preferred_element_type=F32)
    tot = jnp.sum(dta, axis=0, keepdims=True)
    dec = jnp.exp(tot - acs)
    eacs = jnp.exp(acs)
    etot = jnp.exp(tot)
    x = x_ref[0]
    for g in range(SSM_GROUPS):
        gs = slice(g * SSM_STATE, (g + 1) * SSM_STATE)
        cm = c_ref[0, :, gs].astype(BF16)
        bm = b_ref[0, :, gs].astype(BF16)
        bt = bt_ref[0, gs, :].astype(BF16)
        cb = _dot_nt(cm, bm)
        for eh in range(SSM_HPG):
            e = g * SSM_HPG + eh
            es = slice(e * HEAD_DIM, (e + 1) * HEAD_DIM)
            seg = jnp.exp(jnp.where(mask, acs[:, e:e + 1] - acs_t[e:e + 1, :], MASK_NEG))
            xdt = x[:, es] * dt[:, e:e + 1]
            y_diag = _dot((cb * seg).astype(BF16), xdt.astype(BF16))
            state = st_ref[e]
            y_off = _dot(cm, state.astype(BF16)) * eacs[:, e:e + 1]
            y_ref[0, 0, :, es] = y_diag + y_off
            contrib = _dot(bt, (xdt * dec[:, e:e + 1]).astype(BF16))
            st_ref[e] = state * etot[:, e:e + 1] + contrib


def ssd_scan(xs, bm, cm, dt, a, ctx_len):
    bsz, t, _ = xs.shape
    q = SSM_CHUNK
    nc = t // q
    nctx = ctx_len // q
    bt = jnp.swapaxes(bm, 1, 2)
    dtt = jnp.swapaxes(dt, 2, 3)

    def cidx(d, j):
        bwd = jnp.where(j < nctx, nctx - 1 - j, nc - 1 - (j - nctx))
        return jnp.where(d == 0, j, bwd)

    return pl.pallas_call(
        _ssd_kernel,
        grid=(bsz, 2, nc),
        in_specs=[pl.BlockSpec((1, q, SSM_W), lambda b, d, j: (b, cidx(d, j), 0)),
                  pl.BlockSpec((1, SSM_GN, q), lambda b, d, j: (b, 0, cidx(d, j))),
                  pl.BlockSpec((1, q, SSM_GN), lambda b, d, j: (b, cidx(d, j), 0)),
                  pl.BlockSpec((1, q, SSM_GN), lambda b, d, j: (b, cidx(d, j), 0)),
                  pl.BlockSpec((1, 1, q, SSM_HEADS), lambda b, d, j: (b, d, cidx(d, j), 0)),
                  pl.BlockSpec((1, 1, SSM_HEADS, q), lambda b, d, j: (b, d, 0, cidx(d, j))),
                  pl.BlockSpec((1, 1, SSM_HEADS), lambda b, d, j: (d, 0, 0)),
                  pl.BlockSpec((1, SSM_HEADS, 1), lambda b, d, j: (d, 0, 0))],
        out_specs=pl.BlockSpec((1, 1, q, SSM_W), lambda b, d, j: (b, d, cidx(d, j), 0)),
        out_shape=jax.ShapeDtypeStruct((bsz, 2, t, SSM_W), F32),
        scratch_shapes=[pltpu.VMEM((SSM_HEADS, SSM_STATE, HEAD_DIM), F32)],
        compiler_params=_cparams("parallel", "arbitrary", "arbitrary"),
        name="ssd_scan",
    )(xs, bt, cm, bm, dt, dtt, a.reshape(2, 1, SSM_HEADS), a.reshape(2, SSM_HEADS, 1))


def _dwconv(u, w, b):
    k = w.shape[0]
    t = u.shape[1]
    up = jnp.pad(u, ((0, 0), (k // 2, k // 2), (0, 0)))
    y = b
    for i in range(k):
        y = y + up[:, i:i + t] * w[i]
    return y


def ssm_mixer(u, ctx_len, conv_w, conv_b, dt_bias, a_log, d_skip, norm_w):
    bsz, t, _ = u.shape
    z = u[..., :SSM_W]
    xr = u[..., SSM_W:SSM_W + SSM_XBC]
    xbc = jnp.concatenate([_dwconv(xr[:, :ctx_len], conv_w, conv_b),
                           _dwconv(xr[:, ctx_len:], conv_w, conv_b)], axis=1)
    xbc = xbc * jax.nn.sigmoid(xbc)
    xs = xbc[..., :SSM_W]
    bm = xbc[..., SSM_W:SSM_W + SSM_GN]
    cm = xbc[..., SSM_W + SSM_GN:]
    dt_raw = u[..., SSM_W + SSM_XBC:].reshape(bsz, t, 2, SSM_HEADS)
    dt = jax.nn.softplus(dt_raw + dt_bias.reshape(2, SSM_HEADS))
    dt = jnp.transpose(dt, (0, 2, 1, 3))
    a = -jnp.exp(a_log.astype(F32)).reshape(2, SSM_HEADS)
    y2 = ssd_scan(xs, bm, cm, dt, a, ctx_len)
    y = y2[:, 0] + y2[:, 1] + jnp.repeat(d_skip, HEAD_DIM) * xs
    y = y * (z * jax.nn.sigmoid(z))
    y = y.reshape(bsz, t, SSM_GROUPS, SSM_HPG * HEAD_DIM)
    y = y * lax.rsqrt(jnp.mean(jnp.square(y), -1, keepdims=True) + RMS_EPS)
    return (y.reshape(bsz, t, SSM_W) * norm_w).astype(BF16)


def _adaln_router_kernel(x_ref, sh_ref, sc_ref, wr_ref, f_ref, lg_ref):
    xm = x_ref[...] * (1.0 + sc_ref[0]) + sh_ref[0]
    f_ref[...] = xm.astype(f_ref.dtype)
    lg_ref[...] = jnp.dot(xm, wr_ref[...], precision=lax.Precision.HIGHEST, preferred_element_type=F32)


def adaln_router(x, mod_tab, sh_idx, sc_idx, wr):
    nt, d = x.shape
    n = wr.shape[1]
    return pl.pallas_call(
        _adaln_router_kernel,
        grid=(nt // TM,),
        in_specs=[pl.BlockSpec((TM, d), lambda i: (i, 0))] + _mod_specs(d, sh_idx, sc_idx)
        + [pl.BlockSpec((d, n), lambda i: (0, 0))],
        out_specs=[pl.BlockSpec((TM, d), lambda i: (i, 0)), pl.BlockSpec((TM, n), lambda i: (i, 0))],
        out_shape=[jax.ShapeDtypeStruct((nt, d), BF16), jax.ShapeDtypeStruct((nt, n), F32)],
        compiler_params=_cparams("parallel"),
        name="adaln_router",
    )(x, mod_tab, mod_tab, wr)


def _moe_up_kernel(be_ref, x_ref, wg_ref, wu_ref, o_ref):
    x = x_ref[...]
    g = _dot(x, wg_ref[0])
    u = _dot(x, wu_ref[0])
    o_ref[...] = (g * jax.nn.sigmoid(g) * u).astype(o_ref.dtype)


def moe_up(xb, block_e, wg, wu, n_split=2):
    ns, d = xb.shape
    f = wg.shape[2]
    tn = f // n_split
    return pl.pallas_call(
        _moe_up_kernel,
        grid_spec=pltpu.PrefetchScalarGridSpec(
            num_scalar_prefetch=1,
            grid=(n_split, ns // MOE_BLOCK),
            in_specs=[pl.BlockSpec((MOE_BLOCK, d), lambda n, i, be: (i, 0)),
                      pl.BlockSpec((1, d, tn), lambda n, i, be: (be[i], 0, n)),
                      pl.BlockSpec((1, d, tn), lambda n, i, be: (be[i], 0, n))],
            out_specs=pl.BlockSpec((MOE_BLOCK, tn), lambda n, i, be: (i, n))),
        out_shape=jax.ShapeDtypeStruct((ns, f), BF16),
        compiler_params=_cparams("parallel", "arbitrary"),
        name="moe_up",
    )(block_e, xb, wg, wu)


def _moe_down_kernel(be_ref, a_ref, wd_ref, o_ref):
    o_ref[...] = _dot(a_ref[...], wd_ref[0])


def moe_down(act, block_e, wd):
    ns, f = act.shape
    d = wd.shape[2]
    return pl.pallas_call(
        _moe_down_kernel,
        grid_spec=pltpu.PrefetchScalarGridSpec(
            num_scalar_prefetch=1,
            grid=(ns // MOE_BLOCK,),
            in_specs=[pl.BlockSpec((MOE_BLOCK, f), lambda i, be: (i, 0)),
                      pl.BlockSpec((1, f, d), lambda i, be: (be[i], 0, 0))],
            out_specs=pl.BlockSpec((MOE_BLOCK, d), lambda i, be: (i, 0))),
        out_shape=jax.ShapeDtypeStruct((ns, d), F32),
        compiler_params=_cparams("arbitrary"),
        name="moe_down",
    )(block_e, act, wd)


def _moe_res_ln_kernel(h_ref, y0_ref, y1_ref, gt_ref, gate_ref, g_ref, b_ref, out_ref):
    gt = gt_ref[...]
    y = y0_ref[...] * gt[:, 0:1] + y1_ref[...] * gt[:, 1:2]
    z = DEEPNORM_ALPHA * h_ref[...] + gate_ref[0] * y
    out_ref[...] = _layer_norm(z, g_ref[...], b_ref[...])


def moe_res_ln(h, y0, y1, gates, mod_tab, gate_idx, ln_g, ln_b):
    nt, d = h.shape
    row = pl.BlockSpec((TM, d), lambda i: (i, 0))
    return pl.pallas_call(
        _moe_res_ln_kernel,
        grid=(nt // TM,),
        in_specs=[row, row, row,
                  pl.BlockSpec((TM, TOP_K), lambda i: (i, 0)),
                  pl.BlockSpec((1, 1, d), lambda i: (i, 0, gate_idx)),
                  pl.BlockSpec((1, d), lambda i: (0, 0)),
                  pl.BlockSpec((1, d), lambda i: (0, 0))],
        out_specs=row,
        out_shape=jax.ShapeDtypeStruct((nt, d), F32),
        compiler_params=_cparams("parallel"),
        name="moe_res_ln",
    )(h, y0, y1, gates, mod_tab, ln_g.reshape(1, d), ln_b.reshape(1, d))


def moe_mixer(h, mod_tab, router_w, router_b, wg, wu, wd, ln_g, ln_b):
    n_tok, d = h.shape
    wr = jnp.pad(router_w, ((0, 0), (0, ROUTER_PAD - N_EXPERTS)))
    f_bf, logits = adaln_router(h, mod_tab, 3, 4, wr)
    logits = logits[:, :N_EXPERTS] + router_b
    top_val, top_idx = lax.top_k(logits, TOP_K)
    gates = jax.nn.softmax(top_val, axis=-1)
    n_assign = n_tok * TOP_K
    flat_e = top_idx.reshape(-1)
    onehot = (flat_e[:, None] == jnp.arange(N_EXPERTS)[None, :]).astype(jnp.int32)
    csum = jnp.cumsum(onehot, axis=0)
    counts = csum[-1]
    rank = jnp.sum((csum - onehot) * onehot, axis=1)
    padded = (counts + MOE_BLOCK - 1) // MOE_BLOCK * MOE_BLOCK
    pad_end = jnp.cumsum(padded)
    pad_start = pad_end - padded
    dest = pad_start[flat_e] + rank
    n_blocks = -(-n_assign // MOE_BLOCK) + N_EXPERTS
    flat_tok = jnp.repeat(jnp.arange(n_tok, dtype=jnp.int32), TOP_K)
    slot_tok = jnp.full((n_blocks * MOE_BLOCK,), n_tok, jnp.int32).at[dest].set(flat_tok)
    block_e = jnp.minimum(jnp.searchsorted(pad_end, jnp.arange(n_blocks) * MOE_BLOCK, side='right'),
                          N_EXPERTS - 1).astype(jnp.int32)
    f_pad = jnp.concatenate([f_bf, jnp.zeros((1, d), f_bf.dtype)], axis=0)
    xb = f_pad[slot_tok]
    act = moe_up(xb, block_e, wg.astype(BF16), wu.astype(BF16))
    yb = moe_down(act, block_e, wd.astype(BF16))
    dest2 = dest.reshape(n_tok, TOP_K)
    return moe_res_ln(h, yb[dest2[:, 0]], yb[dest2[:, 1]], gates, mod_tab, 5, ln_g, ln_b)


def kernel(x, c, ctx, c_ctx, w_mod, b_mod, w_in, w_out, ln1_g, ln1_b, ln2_g, ln2_b, na_rpb, rw_mu, rw_w0, rw_w2, rw_a0, rw_a2, rw_g2, rw_k_k, rw_k_a, rw_r_k, rw_ln_w, rw_ln_b, ssm_conv_w, ssm_conv_b, ssm_dt_bias, ssm_a_log, ssm_d, ssm_norm_w, ffn_w_gate, ffn_w_up, ffn_w_down, moe_router, moe_router_b, moe_w_gate, moe_w_up, moe_w_down):
    bsz, seq, d = x.shape
    ctx_len = ctx.shape[1]
    tp = ctx_len + seq
    nt = bsz * tp
    assert ctx_len % TM == 0 and seq % TM == 0 and 2 * 2 * bsz * RW_HEADS == LANES
    rows = seq // GRID_W
    depth = w_mod.shape[0]

    h = jnp.concatenate([ctx, x], axis=1).reshape(nt, d)

    n_mod_rows = -(-(bsz + 1) // 16) * 16
    cin = jnp.concatenate([c, c_ctx[None, :], jnp.zeros((n_mod_rows - bsz - 1, d), F32)], axis=0)
    mod_all = modulation(cin, w_mod, b_mod)
    blk = np.arange(nt // TM)
    blk_row = np.where(blk % (tp // TM) < ctx_len // TM, bsz, blk // (tp // TM))

    for l in range(depth):
        mod_tab = mod_all[l][blk_row][:, None, :]
        wl = w_in[l]
        w_in_p = jnp.concatenate(
            [wl[:, :NA_COLS],
             jnp.pad(wl[:, NA_COLS:NA_COLS + RW_COLS], ((0, 0), (0, RW_PAD - RW_COLS))),
             jnp.pad(wl[:, NA_COLS + RW_COLS:], ((0, 0), (0, SSM_PAD - SSM_COLS)))], axis=1).astype(BF16)
        ua, ub, uc = adaln_proj(h, mod_tab, 0, 1, w_in_p, (NA_COLS, RW_PAD, SSM_PAD), (BF16, F32, F32))
        oa = na_attention(ua.reshape(bsz, tp, NA_COLS), _na_bias_table(na_rpb[l], rows), ctx_len)
        ob = rwkv_mixer(ub.reshape(bsz, tp, RW_PAD)[..., :RW_COLS], ctx_len, rw_mu[l], rw_w0[l], rw_w2[l],
                        rw_a0[l], rw_a2[l], rw_g2[l], rw_k_k[l], rw_k_a[l], rw_r_k[l], rw_ln_w[l], rw_ln_b[l])
        oc = ssm_mixer(uc.reshape(bsz, tp, SSM_PAD)[..., :SSM_COLS], ctx_len, ssm_conv_w[l], ssm_conv_b[l],
                       ssm_dt_bias[l], ssm_a_log[l], ssm_d[l], ssm_norm_w[l])
        o = jnp.concatenate([oa, ob, oc], axis=-1).reshape(nt, d)
        h = proj_res_ln(o, w_out[l].astype(BF16), h, mod_tab, 2, ln1_g[l], ln1_b[l])

        j = l // 2
        if l % 2 == 0:
            act = adaln_swiglu(h, mod_tab, 3, 4, ffn_w_gate[j].astype(BF16), ffn_w_up[j].astype(BF16))
            h = proj_res_ln(act, ffn_w_down[j].astype(BF16), h, mod_tab, 5, ln2_g[l], ln2_b[l])
        else:
            h = moe_mixer(h, mod_tab, moe_router[j], moe_router_b[j], moe_w_gate[j], moe_w_up[j],
                          moe_w_down[j], ln2_g[l], ln2_b[l])
    return h.reshape(bsz, tp, d)[:, ctx_len:]
```

```python
import functools

import numpy as np
import jax
import jax.numpy as jnp
from jax import lax
from jax.experimental import pallas as pl
from jax.experimental.pallas import tpu as pltpu

F32 = jnp.float32
BF16 = jnp.bfloat16

D_MODEL = 1024
DEPTH = 4
GRID_W = 64
HEAD_DIM = 64
NA_HEADS = 4
RW_HEADS = 4
SSM_HEADS = 8
NA_W = NA_HEADS * HEAD_DIM
RW_W = RW_HEADS * HEAD_DIM
SSM_W = SSM_HEADS * HEAD_DIM
NA_WIN_R = 8
NA_WIN_C = 16
RW_DECAY_LORA = 32
RW_AAA_LORA = 32
RW_GATE_LORA = 64
RW_GN_EPS = 64e-5
SSM_GROUPS = 2
SSM_HPG = SSM_HEADS // SSM_GROUPS
SSM_STATE = 128
SSM_CONV = 5
SSM_CHUNK = 128
SSM_GN = SSM_GROUPS * SSM_STATE
SSM_XBC = SSM_W + 2 * SSM_GN
NA_COLS = 3 * NA_W
RW_COLS = 3 * RW_W + 2 * RW_DECAY_LORA + 2 * RW_AAA_LORA + RW_GATE_LORA
SSM_COLS = SSM_W + SSM_XBC + 2 * SSM_HEADS
N_EXPERTS = 8
TOP_K = 2
MOE_BLOCK = 256
DEEPNORM_ALPHA = (2 * DEPTH) ** 0.25
LN_EPS = 1e-5
RMS_EPS = 1e-5

LANES = 128
TM = 256
RW_PAD = 1024
SSM_PAD = 1664
ROUTER_PAD = LANES
VMEM_LIMIT = 56 * 1024 * 1024
MASK_NEG = -1e30


def _cparams(*sem):
    return pltpu.CompilerParams(dimension_semantics=sem, vmem_limit_bytes=VMEM_LIMIT)


def _dot(a, b):
    return jnp.dot(a, b, preferred_element_type=F32)


def _dot_nt(a, b):
    return lax.dot_general(a, b, (((1,), (1,)), ((), ())), preferred_element_type=F32)


def _layer_norm(z, g, b):
    mu = jnp.mean(z, -1, keepdims=True)
    zc = z - mu
    var = jnp.mean(zc * zc, -1, keepdims=True)
    return zc * lax.rsqrt(var + LN_EPS) * g + b


def _mod_kernel(c_ref, w_ref, b_ref, o_ref):
    cs = c_ref[...]
    cs = cs * jax.nn.sigmoid(cs)
    o_ref[0] = _dot(cs.astype(BF16), w_ref[0].astype(BF16)) + b_ref[0]


def modulation(cin, w_mod, b_mod):
    r, d = cin.shape
    nl, _, n = w_mod.shape
    tn = d
    return pl.pallas_call(
        _mod_kernel,
        grid=(nl, n // tn),
        in_specs=[pl.BlockSpec((r, d), lambda l, j: (0, 0)),
                  pl.BlockSpec((1, d, tn), lambda l, j: (l, 0, j)),
                  pl.BlockSpec((1, 1, tn), lambda l, j: (l, 0, j))],
        out_specs=pl.BlockSpec((1, r, tn), lambda l, j: (l, 0, j)),
        out_shape=jax.ShapeDtypeStruct((nl, r, n), F32),
        compiler_params=_cparams("parallel", "parallel"),
        name="modulation",
    )(cin, w_mod, b_mod.reshape(nl, 1, n))


def _mod_specs(d, idx_a, idx_b):
    return [pl.BlockSpec((1, 1, d), lambda i: (i, 0, idx_a)),
            pl.BlockSpec((1, 1, d), lambda i: (i, 0, idx_b))]


def _adaln_proj_kernel(x_ref, sh_ref, sc_ref, w_ref, *o_refs, splits):
    xm = x_ref[...] * (1.0 + sc_ref[0]) + sh_ref[0]
    y = _dot(xm.astype(BF16), w_ref[...])
    off = 0
    for o_ref, n in zip(o_refs, splits):
        o_ref[...] = y[:, off:off + n].astype(o_ref.dtype)
        off += n


def adaln_proj(x, mod_tab, sh_idx, sc_idx, w, splits, dtypes):
    nt, d = x.shape
    n = w.shape[1]
    return pl.pallas_call(
        functools.partial(_adaln_proj_kernel, splits=splits),
        grid=(nt // TM,),
        in_specs=[pl.BlockSpec((TM, d), lambda i: (i, 0))] + _mod_specs(d, sh_idx, sc_idx)
        + [pl.BlockSpec((d, n), lambda i: (0, 0))],
        out_specs=[pl.BlockSpec((TM, s), lambda i: (i, 0)) for s in splits],
        out_shape=[jax.ShapeDtypeStruct((nt, s), dt) for s, dt in zip(splits, dtypes)],
        compiler_params=_cparams("parallel"),
        name="adaln_proj",
    )(x, mod_tab, mod_tab, w)


def _adaln_swiglu_kernel(x_ref, sh_ref, sc_ref, wg_ref, wu_ref, o_ref):
    xm = (x_ref[...] * (1.0 + sc_ref[0]) + sh_ref[0]).astype(BF16)
    g = _dot(xm, wg_ref[...])
    u = _dot(xm, wu_ref[...])
    o_ref[...] = (g * jax.nn.sigmoid(g) * u).astype(o_ref.dtype)


def adaln_swiglu(x, mod_tab, sh_idx, sc_idx, wg, wu):
    nt, d = x.shape
    f = wg.shape[1]
    return pl.pallas_call(
        _adaln_swiglu_kernel,
        grid=(nt // TM,),
        in_specs=[pl.BlockSpec((TM, d), lambda i: (i, 0))] + _mod_specs(d, sh_idx, sc_idx)
        + [pl.BlockSpec((d, f), lambda i: (0, 0)), pl.BlockSpec((d, f), lambda i: (0, 0))],
        out_specs=pl.BlockSpec((TM, f), lambda i: (i, 0)),
        out_shape=jax.ShapeDtypeStruct((nt, f), BF16),
        compiler_params=_cparams("parallel"),
        name="adaln_swiglu",
    )(x, mod_tab, mod_tab, wg, wu)


def _proj_res_ln_kernel(o_ref, w_ref, h_ref, gate_ref, g_ref, b_ref, out_ref):
    y = _dot(o_ref[...], w_ref[...])
    z = DEEPNORM_ALPHA * h_ref[...] + gate_ref[0] * y
    out_ref[...] = _layer_norm(z, g_ref[...], b_ref[...])


def proj_res_ln(o, w, h, mod_tab, gate_idx, ln_g, ln_b):
    nt, k = o.shape
    d = w.shape[1]
    return pl.pallas_call(
        _proj_res_ln_kernel,
        grid=(nt // TM,),
        in_specs=[pl.BlockSpec((TM, k), lambda i: (i, 0)),
                  pl.BlockSpec((k, d), lambda i: (0, 0)),
                  pl.BlockSpec((TM, d), lambda i: (i, 0)),
                  pl.BlockSpec((1, 1, d), lambda i: (i, 0, gate_idx)),
                  pl.BlockSpec((1, d), lambda i: (0, 0)),
                  pl.BlockSpec((1, d), lambda i: (0, 0))],
        out_specs=pl.BlockSpec((TM, d), lambda i: (i, 0)),
        out_shape=jax.ShapeDtypeStruct((nt, d), F32),
        compiler_params=_cparams("parallel"),
        name="proj_res_ln",
    )(o, w, h, mod_tab, ln_g.reshape(1, d), ln_b.reshape(1, d))


def _mix_proj_res_ln_kernel(oa_ref, ob_ref, oc_ref, w_ref, h_ref, gate_ref, g_ref, b_ref, out_ref):
    s1, s2 = NA_W, NA_W + RW_W
    y = (_dot(oa_ref[...], w_ref[0:s1, :]) + _dot(ob_ref[...], w_ref[s1:s2, :])
         + _dot(oc_ref[...], w_ref[s2:, :]))
    z = DEEPNORM_ALPHA * h_ref[...] + gate_ref[0] * y
    out_ref[...] = _layer_norm(z, g_ref[...], b_ref[...])


def mix_proj_res_ln(oa, ob, oc, w, h, mod_tab, gate_idx, ln_g, ln_b):
    nt, d = h.shape
    row = lambda n: pl.BlockSpec((TM, n), lambda i: (i, 0))
    return pl.pallas_call(
        _mix_proj_res_ln_kernel,
        grid=(nt // TM,),
        in_specs=[row(NA_W), row(RW_W), row(SSM_W),
                  pl.BlockSpec((d, d), lambda i: (0, 0)),
                  row(d),
                  pl.BlockSpec((1, 1, d), lambda i: (i, 0, gate_idx)),
                  pl.BlockSpec((1, d), lambda i: (0, 0)),
                  pl.BlockSpec((1, d), lambda i: (0, 0))],
        out_specs=row(d),
        out_shape=jax.ShapeDtypeStruct((nt, d), F32),
        compiler_params=_cparams("parallel"),
        name="mix_proj_res_ln",
    )(oa, ob, oc, w, h, mod_tab, ln_g.reshape(1, d), ln_b.reshape(1, d))


def _na_bias_table(rpb, rows):
    win_r = min(NA_WIN_R, rows)
    q = np.arange(GRID_W)
    col_start = np.clip(q - NA_WIN_C // 2, 0, GRID_W - NA_WIN_C)
    kc = np.arange(GRID_W)
    valid = (kc[None, :] >= col_start[:, None]) & (kc[None, :] < col_start[:, None] + NA_WIN_C)
    crel = np.clip(kc[None, :] - q[:, None] + NA_WIN_C - 1, 0, 2 * NA_WIN_C - 2)
    case = np.arange(NA_WIN_R)
    rrel = np.clip(NA_WIN_R - 1 - case[:, None] + np.arange(win_r)[None, :], 0, 2 * NA_WIN_R - 2)
    t = rpb[:, rrel]
    t = t[:, :, :, crel]
    t = jnp.transpose(t, (1, 0, 3, 2, 4))
    t = jnp.where(valid[None, None, :, None, :], t, MASK_NEG)
    return t.reshape(NA_WIN_R, NA_HEADS, GRID_W, win_r * GRID_W).astype(F32)


def _attn_heads(q, k_list, v_list, bias_list, scale):
    outs = []
    for h in range(NA_HEADS):
        sl = slice(h * HEAD_DIM, (h + 1) * HEAD_DIM)
        qh = q[:, sl]
        scores = []
        for kk, bias in zip(k_list, bias_list):
            s = _dot_nt(qh, kk[:, sl]) * scale
            if bias is not None:
                s = s + bias[0, h]
            scores.append(s)
        m = scores[0].max(-1, keepdims=True)
        for s in scores[1:]:
            m = jnp.maximum(m, s.max(-1, keepdims=True))
        den = 0.0
        acc = 0.0
        for s, vv in zip(scores, v_list):
            p = jnp.exp(s - m)
            den = den + p.sum(-1, keepdims=True)
            acc = acc + _dot(p.astype(BF16), vv[:, sl])
        outs.append(acc / den)
    return jnp.concatenate(outs, axis=-1)


def _na_kernel(u_ref, bias_ref, o_ref, *, ctx_len, rows, scale):
    j = pl.program_id(1)
    n_ctx_blk = ctx_len // GRID_W
    win_r = min(NA_WIN_R, rows)
    kc = u_ref[0, 0:ctx_len, NA_W:2 * NA_W]
    vc = u_ref[0, 0:ctx_len, 2 * NA_W:3 * NA_W]

    @pl.when(j < n_ctx_blk)
    def _ctx():
        q = u_ref[0, pl.ds(pl.multiple_of(j * GRID_W, GRID_W), GRID_W), 0:NA_W]
        o_ref[0] = _attn_heads(q, [kc], [vc], [None], scale).astype(o_ref.dtype)

    @pl.when(j >= n_ctx_blk)
    def _lat():
        r = j - n_ctx_blk
        r0 = jnp.clip(r - win_r // 2, 0, rows - win_r)
        q = u_ref[0, pl.ds(pl.multiple_of(ctx_len + r * GRID_W, GRID_W), GRID_W), 0:NA_W]
        kstart = pl.multiple_of(ctx_len + r0 * GRID_W, GRID_W)
        kl = u_ref[0, pl.ds(kstart, win_r * GRID_W), NA_W:2 * NA_W]
        vl = u_ref[0, pl.ds(kstart, win_r * GRID_W), 2 * NA_W:3 * NA_W]
        o_ref[0] = _attn_heads(q, [kl, kc], [vl, vc], [bias_ref, None], scale).astype(o_ref.dtype)


def na_attention(ua, bias_tab, ctx_len):
    bsz, tp, _ = ua.shape
    rows = (tp - ctx_len) // GRID_W
    win_r = min(NA_WIN_R, rows)
    n_ctx_blk = ctx_len // GRID_W

    def bias_idx(b, j):
        r = jnp.clip(j - n_ctx_blk, 0, rows - 1)
        r0 = jnp.clip(r - win_r // 2, 0, rows - win_r)
        return (r - r0, 0, 0, 0)

    return pl.pallas_call(
        functools.partial(_na_kernel, ctx_len=ctx_len, rows=rows, scale=HEAD_DIM ** -0.5),
        grid=(bsz, tp // GRID_W),
        in_specs=[pl.BlockSpec((1, tp, NA_COLS), lambda b, j: (b, 0, 0)),
                  pl.BlockSpec((1, NA_HEADS, GRID_W, win_r * GRID_W), bias_idx)],
        out_specs=pl.BlockSpec((1, GRID_W, NA_W), lambda b, j: (b, j, 0)),
        out_shape=jax.ShapeDtypeStruct((bsz, tp, NA_W), BF16),
        compiler_params=_cparams("parallel", "arbitrary"),
        name="na_attention",
    )(ua, bias_tab)


RW_KQ = 4
RW_KJ = HEAD_DIM // RW_KQ
RW_TBLK = 64
HALO = 8


def _segment_edges(i, blocks_per_batch, ctx_blocks):
    j = i % blocks_per_batch
    first = jnp.logical_or(j == 0, j == ctx_blocks)
    last = jnp.logical_or(j == ctx_blocks - 1, j == blocks_per_batch - 1)
    return first, last


def _with_halo(x, prev, nxt, first, last):
    return jnp.concatenate([jnp.where(first, 0.0, prev), x, jnp.where(last, 0.0, nxt)], axis=0)


def _shifted(xe, s):
    return pltpu.roll(xe, s % xe.shape[0], 0)[HALO:HALO + TM]


def _halo_specs(nt, width):
    return [pl.BlockSpec((TM, width), lambda i: (i, 0)),
            pl.BlockSpec((HALO, width), lambda i: (jnp.maximum(i * (TM // HALO) - 1, 0), 0)),
            pl.BlockSpec((HALO, width), lambda i: (jnp.minimum((i + 1) * (TM // HALO), nt // HALO - 1), 0))]


def _softplus(x):
    return jnp.maximum(x, 0.0) + jnp.log1p(jnp.exp(-jnp.abs(x)))


def _dot_f32(a, b):
    return jnp.dot(a, b, precision=lax.Precision.HIGHEST, preferred_element_type=F32)


def _rwkv_prep_kernel(u_ref, up_ref, un_ref, mu_ref, lw_ref, g2_ref, w0_ref, a0_ref, kk_ref, ka_ref, rk_ref,
                      ones_ref, a_out, r_out, v_out, zf_out, zb_out, gb_out, *, blocks_per_batch, ctx_blocks):
    first, last = _segment_edges(pl.program_id(0), blocks_per_batch, ctx_blocks)
    xe = _with_halo(u_ref[...], up_ref[...], un_ref[...], first, last)
    x = u_ref[...]
    um = x + mu_ref[...] * (0.5 * (_shifted(xe, 1) + _shifted(xe, -1)) - x)
    w = RW_W
    r, k, v = um[:, 0:w], um[:, w:2 * w], um[:, 2 * w:3 * w]
    lo = um[:, 3 * w:3 * w + LANES]
    lane = lax.broadcasted_iota(jnp.int32, lo.shape, 1)
    lo = jnp.where(lane < 2 * RW_DECAY_LORA, jnp.tanh(lo), lo)
    lora = _dot_f32(lo, lw_ref[...])
    g = _dot_f32(jax.nn.sigmoid(um[:, 3 * w + LANES:3 * w + 2 * LANES]), g2_ref[...])
    wlog = -_softplus(-(w0_ref[...] + lora[:, :2 * w])) - 0.5
    decay = jnp.exp(-jnp.exp(wlog))
    aa = jax.nn.sigmoid(a0_ref[...] + lora[:, 2 * w:])
    ones = ones_ref[...]
    kk = k * kk_ref[...]
    kk = kk * lax.rsqrt(_dot_f32(kk * kk, ones) + 1e-12)
    ka = ka_ref[...]
    kd_f = k * (1.0 + (aa[:, :w] - 1.0) * ka)
    kd_b = k * (1.0 + (aa[:, w:] - 1.0) * ka)
    a_out[...] = -kk
    r_out[...] = r
    v_out[...] = v
    zf_out[:, 0:w] = decay[:, :w]
    zf_out[:, w:2 * w] = kk * aa[:, :w]
    zf_out[:, 2 * w:] = kd_f
    zb_out[:, 0:w] = decay[:, w:]
    zb_out[:, w:2 * w] = kk * aa[:, w:]
    zb_out[:, 2 * w:] = kd_b
    gb_out[:, 0:w] = g
    gb_out[:, w:] = _dot_f32(r * (kd_f + kd_b) * rk_ref[...], ones) * v


def rwkv_prep(ub, blocks_per_batch, ctx_blocks, mu, w0, w2, a0, a2, g2, k_k, k_a, r_k):
    nt = ub.shape[0]
    w = RW_W
    lw = jnp.zeros((LANES, 4 * w), F32)
    lw = lw.at[0:32, 0:w].set(w2[0]).at[32:64, w:2 * w].set(w2[1])
    lw = lw.at[64:96, 2 * w:3 * w].set(a2[0]).at[96:128, 3 * w:].set(a2[1])
    g2p = jnp.pad(g2, ((0, LANES - RW_GATE_LORA), (0, 0)))
    mup = jnp.pad(mu, (0, RW_PAD - RW_COLS)).reshape(1, RW_PAD)
    head = np.arange(w) // HEAD_DIM
    ones = jnp.asarray((head[:, None] == head[None, :]).astype(np.float32))
    const = lambda shape: pl.BlockSpec(shape, lambda i: (0, 0))
    row = lambda n: pl.BlockSpec((TM, n), lambda i: (i, 0))
    return pl.pallas_call(
        functools.partial(_rwkv_prep_kernel, blocks_per_batch=blocks_per_batch, ctx_blocks=ctx_blocks),
        grid=(nt // TM,),
        in_specs=_halo_specs(nt, RW_PAD) + [const((1, RW_PAD)), const((LANES, 4 * w)), const((LANES, w)),
                                            const((1, 2 * w)), const((1, 2 * w)), const((1, w)), const((1, w)),
                                            const((1, w)), const((w, w))],
        out_specs=[row(w), row(w), row(w), row(3 * w), row(3 * w), row(2 * w)],
        out_shape=[jax.ShapeDtypeStruct((nt, n), F32) for n in (w, w, w, 3 * w, 3 * w, 2 * w)],
        compiler_params=_cparams("parallel"),
        name="rwkv_prep",
    )(ub, ub, ub, mup, lw, g2p, w0.reshape(1, 2 * w), a0.reshape(1, 2 * w), k_k.reshape(1, w),
      k_a.reshape(1, w), r_k.reshape(1, w), ones)


def _rwkv_scan_kernel(af_ref, afn_ref, ab_ref, abn_ref, rf_ref, rb_ref, vf_ref, vb_ref, zf_ref, zb_ref,
                      yf_ref, yb_ref, s_ref, sa_ref, yraw_ref, *, tblk):
    @pl.when(pl.program_id(0) == 0)
    def _init():
        s_ref[...] = jnp.zeros_like(s_ref)
        sa_ref[...] = jnp.zeros_like(sa_ref)
        yraw_ref[...] = jnp.zeros_like(yraw_ref)

    quarter = LANES // RW_KQ

    def over_keys(p):
        ax = p.ndim - 1
        return (p + pltpu.roll(p, quarter, ax) + pltpu.roll(p, 2 * quarter, ax)
                + pltpu.roll(p, 3 * quarter, ax))

    def rows(ref_row):
        return jnp.broadcast_to(ref_row, (HEAD_DIM, LANES))

    def one_direction(d, tc, tn, at_end, a_ref, an_ref, r_ref, v_ref, z_ref, sa):
        v_t = v_ref[tc]
        y = jnp.zeros((HEAD_DIM, LANES), F32)
        sa_next = jnp.zeros((HEAD_DIM, LANES), F32)
        for j in range(RW_KJ):
            js = slice(j, j + 1)
            s_new = (s_ref[d, j] * rows(z_ref[tc, 0, js, :]) + sa * rows(z_ref[tc, 1, js, :])
                     + v_t * rows(z_ref[tc, 2, js, :]))
            s_ref[d, j] = s_new
            y = y + s_new * rows(r_ref[tc, js, :])
            sa_next = sa_next + s_new * rows(jnp.where(at_end, an_ref[0, js, :], a_ref[tn, js, :]))
        return sa_next, y

    lane_y = lax.broadcasted_iota(jnp.int32, (RW_KJ, LANES), 1)
    lower_half = lane_y < 2 * quarter
    even_quarter = (lane_y & quarter) == 0

    def finish_rows(y):
        b0, b1, b2, b3 = (y[q * RW_KJ:(q + 1) * RW_KJ] for q in range(RW_KQ))
        s02 = jnp.where(lower_half, b0, b2) + pltpu.roll(jnp.where(lower_half, b2, b0), 2 * quarter, 1)
        s13 = jnp.where(lower_half, b1, b3) + pltpu.roll(jnp.where(lower_half, b3, b1), 2 * quarter, 1)
        return jnp.where(even_quarter, s02 + pltpu.roll(s02, 3 * quarter, 1), s13 + pltpu.roll(s13, quarter, 1))

    def step(t, carry):
        sa_f, part_b = carry
        tb = tblk - 1 - t
        at_end = t == tblk - 1
        sa_b = over_keys(part_b)
        tb_prev = jnp.minimum(tb + 1, tblk - 1)
        yb_ref[tb_prev] = finish_rows(yraw_ref[tb_prev])
        part_f, y_f = one_direction(0, t, jnp.minimum(t + 1, tblk - 1), at_end, af_ref, afn_ref, rf_ref,
                                    vf_ref, zf_ref, sa_f)
        yf_ref[t] = finish_rows(y_f)
        part_b, y_b = one_direction(1, tb, jnp.maximum(tb - 1, 0), at_end, ab_ref, abn_ref, rb_ref, vb_ref,
                                    zb_ref, sa_b)
        yraw_ref[tb] = y_b
        return over_keys(part_f), part_b

    sa_f, part_b = lax.fori_loop(0, tblk, step, (sa_ref[0], sa_ref[1]))
    sa_ref[0] = sa_f
    sa_ref[1] = part_b
    yb_ref[0] = finish_rows(yraw_ref[0])


def rwkv_scan(a_s, r_s, v_s, zf_s, zb_s, ctx_len, tblk=RW_TBLK):
    t = a_s.shape[0]
    nblk = t // tblk
    ncb = ctx_len // tblk
    kh = RW_KJ

    def rev(i):
        return jnp.where(i < ncb, ncb - 1 - i, nblk - 1 + ncb - i)

    def fwd3(i):
        return (i, 0, 0)

    def bwd3(i):
        return (rev(i), 0, 0)

    return pl.pallas_call(
        functools.partial(_rwkv_scan_kernel, tblk=tblk),
        grid=(nblk,),
        in_specs=[pl.BlockSpec((tblk, kh, LANES), fwd3),
                  pl.BlockSpec((1, kh, LANES), lambda i: (jnp.minimum(i + 1, nblk - 1) * tblk, 0, 0)),
                  pl.BlockSpec((tblk, kh, LANES), bwd3),
                  pl.BlockSpec((1, kh, LANES), lambda i: (rev(jnp.minimum(i + 1, nblk - 1)) * tblk + tblk - 1, 0, 0)),
                  pl.BlockSpec((tblk, kh, LANES), fwd3),
                  pl.BlockSpec((tblk, kh, LANES), bwd3),
                  pl.BlockSpec((tblk, HEAD_DIM, LANES), fwd3),
                  pl.BlockSpec((tblk, HEAD_DIM, LANES), bwd3),
                  pl.BlockSpec((tblk, 3, kh, LANES), lambda i: (i, 0, 0, 0)),
                  pl.BlockSpec((tblk, 3, kh, LANES), lambda i: (rev(i), 0, 0, 0))],
        out_specs=[pl.BlockSpec((tblk, kh, LANES), fwd3), pl.BlockSpec((tblk, kh, LANES), bwd3)],
        out_shape=[jax.ShapeDtypeStruct((t, kh, LANES), F32)] * 2,
        scratch_shapes=[pltpu.VMEM((2, kh, HEAD_DIM, LANES), F32), pltpu.VMEM((2, HEAD_DIM, LANES), F32),
                        pltpu.VMEM((tblk, HEAD_DIM, LANES), F32)],
        compiler_params=_cparams("arbitrary"),
        name="rwkv_scan",
    )(a_s, a_s, a_s, a_s, r_s, r_s, v_s, v_s, zf_s, zb_s)


def _rwkv_post_kernel(yf_ref, yb_ref, gb_ref, lw_ref, lb_ref, ones_ref, o_ref):
    w = RW_W
    ones = ones_ref[...]
    y = yf_ref[...] + yb_ref[...]
    yc = y - _dot_f32(y, ones) * (1.0 / HEAD_DIM)
    var = _dot_f32(yc * yc, ones) * (1.0 / HEAD_DIM)
    yn = yc * lax.rsqrt(var + RW_GN_EPS) * lw_ref[...] + lb_ref[...]
    o_ref[...] = ((yn + gb_ref[:, w:]) * gb_ref[:, 0:w]).astype(o_ref.dtype)


def rwkv_post(yf, yb, gb, ln_w, ln_b):
    nt, w = yf.shape
    head = np.arange(w) // HEAD_DIM
    ones = jnp.asarray((head[:, None] == head[None, :]).astype(np.float32))
    row = lambda n: pl.BlockSpec((TM, n), lambda i: (i, 0))
    const = lambda shape: pl.BlockSpec(shape, lambda i: (0, 0))
    return pl.pallas_call(
        _rwkv_post_kernel,
        grid=(nt // TM,),
        in_specs=[row(w), row(w), row(2 * w), const((1, w)), const((1, w)), const((w, w))],
        out_specs=row(w),
        out_shape=jax.ShapeDtypeStruct((nt, w), BF16),
        compiler_params=_cparams("parallel"),
        name="rwkv_post",
    )(yf, yb, gb, ln_w.reshape(1, w), ln_b.reshape(1, w), ones)


def _rw_keys_to_scan(z, bsz, nq):
    t = z.shape[0] // bsz
    z = z.reshape(bsz, t, nq, RW_HEADS, RW_KQ, RW_KJ)
    z = jnp.transpose(z, (1, 2, 5, 4, 0, 3))
    return z.reshape(t, nq, RW_KJ, LANES)


def _rw_vals_to_scan(v, bsz):
    t = v.shape[0] // bsz
    z = jnp.transpose(v.reshape(bsz, t, RW_HEADS, HEAD_DIM), (1, 3, 0, 2))
    z = jnp.broadcast_to(z[:, :, None], (t, HEAD_DIM, RW_KQ, bsz, RW_HEADS))
    return z.reshape(t, HEAD_DIM, LANES)


def _rw_vals_from_scan(y, bsz):
    t = y.shape[0]
    z = y.reshape(t, RW_KJ, RW_KQ, bsz, RW_HEADS)
    z = jnp.transpose(z, (3, 0, 4, 2, 1))
    return z.reshape(bsz * t, RW_W)


def rwkv_mixer(ub, bsz, ctx_len, mu, w0, w2, a0, a2, g2, k_k, k_a, r_k, ln_w, ln_b, tblk=RW_TBLK):
    tp = ub.shape[0] // bsz
    a_t, r_t, v_t, zf_t, zb_t, gb = rwkv_prep(ub, tp // TM, ctx_len // TM, mu, w0, w2, a0, a2, g2, k_k, k_a, r_k)
    yf, yb = rwkv_scan(_rw_keys_to_scan(a_t, bsz, 1)[:, 0], _rw_keys_to_scan(r_t, bsz, 1)[:, 0],
                       _rw_vals_to_scan(v_t, bsz), _rw_keys_to_scan(zf_t, bsz, 3), _rw_keys_to_scan(zb_t, bsz, 3),
                       ctx_len, tblk)
    return rwkv_post(_rw_vals_from_scan(yf, bsz), _rw_vals_from_scan(yb, bsz), gb, ln_w, ln_b)


def _ssd_kernel(x_ref, bt_ref, c_ref, b_ref, dt_ref, dtt_ref, a_ref, at_ref, y_ref, st_ref):
    d = pl.program_id(1)
    j = pl.program_id(2)
    q = SSM_CHUNK
    hp = lax.Precision.HIGHEST

    @pl.when(j == 0)
    def _init():
        st_ref[...] = jnp.zeros_like(st_ref)

    row = lax.broadcasted_iota(jnp.int32, (q, q), 0)
    col = lax.broadcasted_iota(jnp.int32, (q, q), 1)
    mask = (row - col) * (1 - 2 * d) >= 0
    tri = mask.astype(F32)
    dt = dt_ref[0, 0]
    dta = dt * a_ref[0]
    dta_t = dtt_ref[0, 0] * at_ref[0]
    acs = jnp.dot(tri, dta, precision=hp, preferred_element_type=F32)
    acs_t = lax.dot_general(dta_t, tri, (((1,), (1,)), ((), ())), precision=hp,
                            preferred_element_type=F32)
    tot = jnp.sum(dta, axis=0, keepdims=True)
    dec = jnp.exp(tot - acs)
    eacs = jnp.exp(acs)
    etot = jnp.exp(tot)
    x = x_ref[0]
    for g in range(SSM_GROUPS):
        gs = slice(g * SSM_STATE, (g + 1) * SSM_STATE)
        cm = c_ref[0, :, gs]
        bm = b_ref[0, :, gs]
        bt = bt_ref[0, gs, :]
        cb = _dot_nt(cm, bm)
        for eh in range(SSM_HPG):
            e = g * SSM_HPG + eh
            es = slice(e * HEAD_DIM, (e + 1) * HEAD_DIM)
            seg = jnp.exp(jnp.where(mask, acs[:, e:e + 1] - acs_t[e:e + 1, :], MASK_NEG))
            xdt = x[:, es] * dt[:, e:e + 1]
            y_diag = _dot((cb * seg).astype(BF16), xdt.astype(BF16))
            state = st_ref[e]
            y_off = _dot(cm, state.astype(BF16)) * eacs[:, e:e + 1]
            y_ref[0, 0, :, es] = y_diag + y_off
            contrib = _dot(bt, (xdt * dec[:, e:e + 1]).astype(BF16))
            st_ref[e] = state * etot[:, e:e + 1] + contrib


def ssd_scan(xs, bm, cm, dt, a, ctx_len):
    bsz, t, _ = xs.shape
    q = SSM_CHUNK
    nc = t // q
    nctx = ctx_len // q
    bt = jnp.swapaxes(bm, 1, 2)
    dtt = jnp.swapaxes(dt, 2, 3)

    def cidx(d, j):
        bwd = jnp.where(j < nctx, nctx - 1 - j, nc - 1 - (j - nctx))
        return jnp.where(d == 0, j, bwd)

    return pl.pallas_call(
        _ssd_kernel,
        grid=(bsz, 2, nc),
        in_specs=[pl.BlockSpec((1, q, SSM_W), lambda b, d, j: (b, cidx(d, j), 0)),
                  pl.BlockSpec((1, SSM_GN, q), lambda b, d, j: (b, 0, cidx(d, j))),
                  pl.BlockSpec((1, q, SSM_GN), lambda b, d, j: (b, cidx(d, j), 0)),
                  pl.BlockSpec((1, q, SSM_GN), lambda b, d, j: (b, cidx(d, j), 0)),
                  pl.BlockSpec((1, 1, q, SSM_HEADS), lambda b, d, j: (b, d, cidx(d, j), 0)),
                  pl.BlockSpec((1, 1, SSM_HEADS, q), lambda b, d, j: (b, d, 0, cidx(d, j))),
                  pl.BlockSpec((1, 1, SSM_HEADS), lambda b, d, j: (d, 0, 0)),
                  pl.BlockSpec((1, SSM_HEADS, 1), lambda b, d, j: (d, 0, 0))],
        out_specs=pl.BlockSpec((1, 1, q, SSM_W), lambda b, d, j: (b, d, cidx(d, j), 0)),
        out_shape=jax.ShapeDtypeStruct((bsz, 2, t, SSM_W), F32),
        scratch_shapes=[pltpu.VMEM((SSM_HEADS, SSM_STATE, HEAD_DIM), F32)],
        compiler_params=_cparams("parallel", "arbitrary", "arbitrary"),
        name="ssd_scan",
    )(xs, bt, cm, bm, dt, dtt, a.reshape(2, 1, SSM_HEADS), a.reshape(2, SSM_HEADS, 1))


def _ssm_prep_kernel(u_ref, up_ref, un_ref, cw_ref, cb_ref, dtb_ref, xs_out, b_out, c_out, dt_out,
                     *, blocks_per_batch, ctx_blocks):
    first, last = _segment_edges(pl.program_id(0), blocks_per_batch, ctx_blocks)
    lo, hi = SSM_W, SSM_W + SSM_XBC
    xe = _with_halo(u_ref[:, lo:hi], up_ref[:, lo:hi], un_ref[:, lo:hi], first, last)
    acc = cb_ref[...] + cw_ref[0:1, :] * _shifted(xe, SSM_CONV // 2)
    for i in range(1, SSM_CONV):
        acc = acc + cw_ref[i:i + 1, :] * _shifted(xe, SSM_CONV // 2 - i)
    xbc = acc * jax.nn.sigmoid(acc)
    xs_out[...] = xbc[:, :SSM_W]
    b_out[...] = xbc[:, SSM_W:SSM_W + SSM_GN].astype(b_out.dtype)
    c_out[...] = xbc[:, SSM_W + SSM_GN:].astype(c_out.dtype)
    dt_out[...] = _softplus(u_ref[:, hi:] + dtb_ref[...])


def ssm_prep(uc, blocks_per_batch, ctx_blocks, conv_w, conv_b, dt_bias):
    nt = uc.shape[0]
    npad = SSM_PAD - SSM_W - SSM_XBC
    dtb = jnp.pad(dt_bias.reshape(-1), (0, npad - 2 * SSM_HEADS)).reshape(1, npad)
    const = lambda shape: pl.BlockSpec(shape, lambda i: (0, 0))
    row = lambda n: pl.BlockSpec((TM, n), lambda i: (i, 0))
    return pl.pallas_call(
        functools.partial(_ssm_prep_kernel, blocks_per_batch=blocks_per_batch, ctx_blocks=ctx_blocks),
        grid=(nt // TM,),
        in_specs=_halo_specs(nt, SSM_PAD) + [const((SSM_CONV, SSM_XBC)), const((1, SSM_XBC)), const((1, npad))],
        out_specs=[row(SSM_W), row(SSM_GN), row(SSM_GN), row(npad)],
        out_shape=[jax.ShapeDtypeStruct((nt, SSM_W), F32), jax.ShapeDtypeStruct((nt, SSM_GN), BF16),
                   jax.ShapeDtypeStruct((nt, SSM_GN), BF16), jax.ShapeDtypeStruct((nt, npad), F32)],
        compiler_params=_cparams("parallel"),
        name="ssm_prep",
    )(uc, uc, uc, conv_w, conv_b.reshape(1, SSM_XBC), dtb)


def _ssm_post_kernel(yf_ref, yb_ref, xs_ref, z_ref, d_ref, nw_ref, o_ref):
    z = z_ref[...]
    y = (yf_ref[0, 0] + yb_ref[0, 0] + d_ref[...] * xs_ref[...]) * (z * jax.nn.sigmoid(z))
    gw = SSM_HPG * HEAD_DIM
    for g in range(SSM_GROUPS):
        gs = slice(g * gw, (g + 1) * gw)
        yg = y[:, gs]
        ms = jnp.mean(yg * yg, -1, keepdims=True)
        o_ref[:, gs] = (yg * lax.rsqrt(ms + RMS_EPS) * nw_ref[:, gs]).astype(o_ref.dtype)


def ssm_post(y2, xs, uc, blocks_per_batch, d_skip, norm_w):
    nt = xs.shape[0]
    row = pl.BlockSpec((TM, SSM_W), lambda i: (i, 0))
    const = pl.BlockSpec((1, SSM_W), lambda i: (0, 0))
    ydir = lambda d: pl.BlockSpec((1, 1, TM, SSM_W),
                                  lambda i: (i // blocks_per_batch, d, i % blocks_per_batch, 0))
    return pl.pallas_call(
        _ssm_post_kernel,
        grid=(nt // TM,),
        in_specs=[ydir(0), ydir(1), row, row, const, const],
        out_specs=row,
        out_shape=jax.ShapeDtypeStruct((nt, SSM_W), BF16),
        compiler_params=_cparams("parallel"),
        name="ssm_post",
    )(y2, y2, xs, uc, jnp.repeat(d_skip, HEAD_DIM).reshape(1, SSM_W), norm_w.reshape(1, SSM_W))


def ssm_mixer(uc, bsz, ctx_len, conv_w, conv_b, dt_bias, a_log, d_skip, norm_w):
    nt = uc.shape[0]
    tp = nt // bsz
    xs, bm, cm, dtp = ssm_prep(uc, tp // TM, ctx_len // TM, conv_w, conv_b, dt_bias)
    dt = jnp.transpose(dtp[:, :2 * SSM_HEADS].reshape(bsz, tp, 2, SSM_HEADS), (0, 2, 1, 3))
    a = -jnp.exp(a_log.astype(F32)).reshape(2, SSM_HEADS)
    y2 = ssd_scan(xs.reshape(bsz, tp, SSM_W), bm.reshape(bsz, tp, SSM_GN), cm.reshape(bsz, tp, SSM_GN),
                  dt, a, ctx_len)
    return ssm_post(y2, xs, uc, tp // TM, d_skip, norm_w)


def _adaln_router_kernel(x_ref, sh_ref, sc_ref, wr_ref, f_ref, lg_ref):
    xm = x_ref[...] * (1.0 + sc_ref[0]) + sh_ref[0]
    f_ref[...] = xm.astype(f_ref.dtype)
    lg_ref[...] = jnp.dot(xm, wr_ref[...], precision=lax.Precision.HIGHEST, preferred_element_type=F32)


def adaln_router(x, mod_tab, sh_idx, sc_idx, wr):
    nt, d = x.shape
    n = wr.shape[1]
    return pl.pallas_call(
        _adaln_router_kernel,
        grid=(nt // TM,),
        in_specs=[pl.BlockSpec((TM, d), lambda i: (i, 0))] + _mod_specs(d, sh_idx, sc_idx)
        + [pl.BlockSpec((d, n), lambda i: (0, 0))],
        out_specs=[pl.BlockSpec((TM, d), lambda i: (i, 0)), pl.BlockSpec((TM, n), lambda i: (i, 0))],
        out_shape=[jax.ShapeDtypeStruct((nt, d), BF16), jax.ShapeDtypeStruct((nt, n), F32)],
        compiler_params=_cparams("parallel"),
        name="adaln_router",
    )(x, mod_tab, mod_tab, wr)


def _moe_up_kernel(be_ref, x_ref, wg_ref, wu_ref, o_ref):
    x = x_ref[...]
    g = _dot(x, wg_ref[0])
    u = _dot(x, wu_ref[0])
    o_ref[...] = (g * jax.nn.sigmoid(g) * u).astype(o_ref.dtype)


def moe_up(xb, block_e, wg, wu, n_split=2):
    ns, d = xb.shape
    f = wg.shape[2]
    tn = f // n_split
    return pl.pallas_call(
        _moe_up_kernel,
        grid_spec=pltpu.PrefetchScalarGridSpec(
            num_scalar_prefetch=1,
            grid=(n_split, ns // MOE_BLOCK),
            in_specs=[pl.BlockSpec((MOE_BLOCK, d), lambda n, i, be: (i, 0)),
                      pl.BlockSpec((1, d, tn), lambda n, i, be: (be[i], 0, n)),
                      pl.BlockSpec((1, d, tn), lambda n, i, be: (be[i], 0, n))],
            out_specs=pl.BlockSpec((MOE_BLOCK, tn), lambda n, i, be: (i, n))),
        out_shape=jax.ShapeDtypeStruct((ns, f), BF16),
        compiler_params=_cparams("parallel", "arbitrary"),
        name="moe_up",
    )(block_e, xb, wg, wu)


def _moe_down_kernel(be_ref, a_ref, wd_ref, o_ref):
    o_ref[...] = _dot(a_ref[...], wd_ref[0])


def moe_down(act, block_e, wd):
    ns, f = act.shape
    d = wd.shape[2]
    return pl.pallas_call(
        _moe_down_kernel,
        grid_spec=pltpu.PrefetchScalarGridSpec(
            num_scalar_prefetch=1,
            grid=(ns // MOE_BLOCK,),
            in_specs=[pl.BlockSpec((MOE_BLOCK, f), lambda i, be: (i, 0)),
                      pl.BlockSpec((1, f, d), lambda i, be: (be[i], 0, 0))],
            out_specs=pl.BlockSpec((MOE_BLOCK, d), lambda i, be: (i, 0))),
        out_shape=jax.ShapeDtypeStruct((ns, d), F32),
        compiler_params=_cparams("arbitrary"),
        name="moe_down",
    )(block_e, act, wd)


def _moe_res_ln_kernel(h_ref, y0_ref, y1_ref, gt_ref, gate_ref, g_ref, b_ref, out_ref):
    gt = gt_ref[...]
    y = y0_ref[...] * gt[:, 0:1] + y1_ref[...] * gt[:, 1:2]
    z = DEEPNORM_ALPHA * h_ref[...] + gate_ref[0] * y
    out_ref[...] = _layer_norm(z, g_ref[...], b_ref[...])


def moe_res_ln(h, y0, y1, gates, mod_tab, gate_idx, ln_g, ln_b):
    nt, d = h.shape
    row = pl.BlockSpec((TM, d), lambda i: (i, 0))
    return pl.pallas_call(
        _moe_res_ln_kernel,
        grid=(nt // TM,),
        in_specs=[row, row, row,
                  pl.BlockSpec((TM, TOP_K), lambda i: (i, 0)),
                  pl.BlockSpec((1, 1, d), lambda i: (i, 0, gate_idx)),
                  pl.BlockSpec((1, d), lambda i: (0, 0)),
                  pl.BlockSpec((1, d), lambda i: (0, 0))],
        out_specs=row,
        out_shape=jax.ShapeDtypeStruct((nt, d), F32),
        compiler_params=_cparams("parallel"),
        name="moe_res_ln",
    )(h, y0, y1, gates, mod_tab, ln_g.reshape(1, d), ln_b.reshape(1, d))


def moe_mixer(h, mod_tab, router_w, router_b, wg, wu, wd, ln_g, ln_b):
    n_tok, d = h.shape
    wr = jnp.pad(router_w, ((0, 0), (0, ROUTER_PAD - N_EXPERTS)))
    f_bf, logits = adaln_router(h, mod_tab, 3, 4, wr)
    logits = logits[:, :N_EXPERTS] + router_b
    top_val, top_idx = lax.top_k(logits, TOP_K)
    gates = jax.nn.softmax(top_val, axis=-1)
    n_assign = n_tok * TOP_K
    flat_e = top_idx.reshape(-1)
    onehot = (flat_e[:, None] == jnp.arange(N_EXPERTS)[None, :]).astype(jnp.int32)
    csum = jnp.cumsum(onehot, axis=0)
    counts = csum[-1]
    rank = jnp.sum((csum - onehot) * onehot, axis=1)
    padded = (counts + MOE_BLOCK - 1) // MOE_BLOCK * MOE_BLOCK
    pad_end = jnp.cumsum(padded)
    pad_start = pad_end - padded
    dest = pad_start[flat_e] + rank
    n_blocks = -(-n_assign // MOE_BLOCK) + N_EXPERTS
    flat_tok = jnp.repeat(jnp.arange(n_tok, dtype=jnp.int32), TOP_K)
    slot_tok = jnp.full((n_blocks * MOE_BLOCK,), n_tok, jnp.int32).at[dest].set(flat_tok)
    block_e = jnp.minimum(jnp.searchsorted(pad_end, jnp.arange(n_blocks) * MOE_BLOCK, side='right'),
                          N_EXPERTS - 1).astype(jnp.int32)
    f_pad = jnp.concatenate([f_bf, jnp.zeros((1, d), f_bf.dtype)], axis=0)
    xb = f_pad[slot_tok]
    act = moe_up(xb, block_e, wg.astype(BF16), wu.astype(BF16))
    yb = moe_down(act, block_e, wd.astype(BF16))
    dest2 = dest.reshape(n_tok, TOP_K)
    return moe_res_ln(h, yb[dest2[:, 0]], yb[dest2[:, 1]], gates, mod_tab, 5, ln_g, ln_b)


def kernel(x, c, ctx, c_ctx, w_mod, b_mod, w_in, w_out, ln1_g, ln1_b, ln2_g, ln2_b, na_rpb, rw_mu, rw_w0, rw_w2, rw_a0, rw_a2, rw_g2, rw_k_k, rw_k_a, rw_r_k, rw_ln_w, rw_ln_b, ssm_conv_w, ssm_conv_b, ssm_dt_bias, ssm_a_log, ssm_d, ssm_norm_w, ffn_w_gate, ffn_w_up, ffn_w_down, moe_router, moe_router_b, moe_w_gate, moe_w_up, moe_w_down):
    bsz, seq, d = x.shape
    ctx_len = ctx.shape[1]
    tp = ctx_len + seq
    nt = bsz * tp
    assert ctx_len % TM == 0 and seq % TM == 0 and RW_KQ * bsz * RW_HEADS == LANES
    rows = seq // GRID_W
    depth = w_mod.shape[0]

    h = jnp.concatenate([ctx, x], axis=1).reshape(nt, d)

    n_mod_rows = -(-(bsz + 1) // 16) * 16
    cin = jnp.concatenate([c, c_ctx[None, :], jnp.zeros((n_mod_rows - bsz - 1, d), F32)], axis=0)
    mod_all = modulation(cin, w_mod, b_mod)
    blk = np.arange(nt // TM)
    blk_row = np.where(blk % (tp // TM) < ctx_len // TM, bsz, blk // (tp // TM))

    for l in range(depth):
        mod_tab = mod_all[l][blk_row][:, None, :]
        wl = w_in[l]
        w_in_p = jnp.concatenate(
            [wl[:, :NA_COLS],
             jnp.pad(wl[:, NA_COLS:NA_COLS + RW_COLS], ((0, 0), (0, RW_PAD - RW_COLS))),
             jnp.pad(wl[:, NA_COLS + RW_COLS:], ((0, 0), (0, SSM_PAD - SSM_COLS)))], axis=1).astype(BF16)
        ua, ub, uc = adaln_proj(h, mod_tab, 0, 1, w_in_p, (NA_COLS, RW_PAD, SSM_PAD), (BF16, F32, F32))
        oa = na_attention(ua.reshape(bsz, tp, NA_COLS), _na_bias_table(na_rpb[l], rows), ctx_len)
        ob = rwkv_mixer(ub, bsz, ctx_len, rw_mu[l], rw_w0[l], rw_w2[l], rw_a0[l], rw_a2[l], rw_g2[l],
                        rw_k_k[l], rw_k_a[l], rw_r_k[l], rw_ln_w[l], rw_ln_b[l])
        oc = ssm_mixer(uc, bsz, ctx_len, ssm_conv_w[l], ssm_conv_b[l], ssm_dt_bias[l], ssm_a_log[l],
                       ssm_d[l], ssm_norm_w[l])
        h = mix_proj_res_ln(oa.reshape(nt, NA_W), ob, oc, w_out[l].astype(BF16), h, mod_tab, 2,
                            ln1_g[l], ln1_b[l])

        j = l // 2
        if l % 2 == 0:
            act = adaln_swiglu(h, mod_tab, 3, 4, ffn_w_gate[j].astype(BF16), ffn_w_up[j].astype(BF16))
            h = proj_res_ln(act, ffn_w_down[j].astype(BF16), h, mod_tab, 5, ln2_g[l], ln2_b[l])
        else:
            h = moe_mixer(h, mod_tab, moe_router[j], moe_router_b[j], moe_w_gate[j], moe_w_up[j],
                          moe_w_down[j], ln2_g[l], ln2_b[l])
    return h.reshape(bsz, tp, d)[:, ctx_len:]
```

```python
import functools

import numpy as np
import jax
import jax.numpy as jnp
from jax import lax
from jax.experimental import pallas as pl
from jax.experimental.pallas import tpu as pltpu

F32 = jnp.float32
BF16 = jnp.bfloat16

D_MODEL = 1024
DEPTH = 4
GRID_W = 64
HEAD_DIM = 64
NA_HEADS = 4
RW_HEADS = 4
SSM_HEADS = 8
NA_W = NA_HEADS * HEAD_DIM
RW_W = RW_HEADS * HEAD_DIM
SSM_W = SSM_HEADS * HEAD_DIM
NA_WIN_R = 8
NA_WIN_C = 16
RW_DECAY_LORA = 32
RW_AAA_LORA = 32
RW_GATE_LORA = 64
RW_GN_EPS = 64e-5
SSM_GROUPS = 2
SSM_HPG = SSM_HEADS // SSM_GROUPS
SSM_STATE = 128
SSM_CONV = 5
SSM_CHUNK = 128
SSM_GN = SSM_GROUPS * SSM_STATE
SSM_XBC = SSM_W + 2 * SSM_GN
NA_COLS = 3 * NA_W
RW_COLS = 3 * RW_W + 2 * RW_DECAY_LORA + 2 * RW_AAA_LORA + RW_GATE_LORA
SSM_COLS = SSM_W + SSM_XBC + 2 * SSM_HEADS
N_EXPERTS = 8
TOP_K = 2
MOE_BLOCK = 256
DEEPNORM_ALPHA = (2 * DEPTH) ** 0.25
LN_EPS = 1e-5
RMS_EPS = 1e-5

LANES = 128
TM = 256
RW_PAD = 1024
SSM_PAD = 1664
ROUTER_PAD = LANES
VMEM_LIMIT = 56 * 1024 * 1024
MASK_NEG = -1e30


def _cparams(*sem):
    return pltpu.CompilerParams(dimension_semantics=sem, vmem_limit_bytes=VMEM_LIMIT)


def _dot(a, b):
    return jnp.dot(a, b, preferred_element_type=F32)


def _dot_nt(a, b):
    return lax.dot_general(a, b, (((1,), (1,)), ((), ())), preferred_element_type=F32)


def _layer_norm(z, g, b):
    mu = jnp.mean(z, -1, keepdims=True)
    zc = z - mu
    var = jnp.mean(zc * zc, -1, keepdims=True)
    return zc * lax.rsqrt(var + LN_EPS) * g + b


def _mod_kernel(c_ref, w_ref, b_ref, o_ref):
    cs = c_ref[...]
    cs = cs * jax.nn.sigmoid(cs)
    o_ref[0] = _dot(cs.astype(BF16), w_ref[0].astype(BF16)) + b_ref[0]


def modulation(cin, w_mod, b_mod):
    r, d = cin.shape
    nl, _, n = w_mod.shape
    tn = d
    return pl.pallas_call(
        _mod_kernel,
        grid=(nl, n // tn),
        in_specs=[pl.BlockSpec((r, d), lambda l, j: (0, 0)),
                  pl.BlockSpec((1, d, tn), lambda l, j: (l, 0, j)),
                  pl.BlockSpec((1, 1, tn), lambda l, j: (l, 0, j))],
        out_specs=pl.BlockSpec((1, r, tn), lambda l, j: (l, 0, j)),
        out_shape=jax.ShapeDtypeStruct((nl, r, n), F32),
        compiler_params=_cparams("parallel", "parallel"),
        name="modulation",
    )(cin, w_mod, b_mod.reshape(nl, 1, n))


def _mod_specs(d, idx_a, idx_b):
    return [pl.BlockSpec((1, 1, d), lambda i: (i, 0, idx_a)),
            pl.BlockSpec((1, 1, d), lambda i: (i, 0, idx_b))]


def _adaln_proj_kernel(x_ref, sh_ref, sc_ref, w_ref, *o_refs, splits):
    xm = x_ref[...] * (1.0 + sc_ref[0]) + sh_ref[0]
    y = _dot(xm.astype(BF16), w_ref[...])
    off = 0
    for o_ref, n in zip(o_refs, splits):
        o_ref[...] = y[:, off:off + n].astype(o_ref.dtype)
        off += n


def adaln_proj(x, mod_tab, sh_idx, sc_idx, w, splits, dtypes):
    nt, d = x.shape
    n = w.shape[1]
    return pl.pallas_call(
        functools.partial(_adaln_proj_kernel, splits=splits),
        grid=(nt // TM,),
        in_specs=[pl.BlockSpec((TM, d), lambda i: (i, 0))] + _mod_specs(d, sh_idx, sc_idx)
        + [pl.BlockSpec((d, n), lambda i: (0, 0))],
        out_specs=[pl.BlockSpec((TM, s), lambda i: (i, 0)) for s in splits],
        out_shape=[jax.ShapeDtypeStruct((nt, s), dt) for s, dt in zip(splits, dtypes)],
        compiler_params=_cparams("parallel"),
        name="adaln_proj",
    )(x, mod_tab, mod_tab, w)


def _adaln_swiglu_kernel(x_ref, sh_ref, sc_ref, wg_ref, wu_ref, o_ref):
    xm = (x_ref[...] * (1.0 + sc_ref[0]) + sh_ref[0]).astype(BF16)
    g = _dot(xm, wg_ref[...])
    u = _dot(xm, wu_ref[...])
    o_ref[...] = (g * jax.nn.sigmoid(g) * u).astype(o_ref.dtype)


def adaln_swiglu(x, mod_tab, sh_idx, sc_idx, wg, wu):
    nt, d = x.shape
    f = wg.shape[1]
    return pl.pallas_call(
        _adaln_swiglu_kernel,
        grid=(nt // TM,),
        in_specs=[pl.BlockSpec((TM, d), lambda i: (i, 0))] + _mod_specs(d, sh_idx, sc_idx)
        + [pl.BlockSpec((d, f), lambda i: (0, 0)), pl.BlockSpec((d, f), lambda i: (0, 0))],
        out_specs=pl.BlockSpec((TM, f), lambda i: (i, 0)),
        out_shape=jax.ShapeDtypeStruct((nt, f), BF16),
        compiler_params=_cparams("parallel"),
        name="adaln_swiglu",
    )(x, mod_tab, mod_tab, wg, wu)


def _proj_res_ln_kernel(o_ref, w_ref, h_ref, gate_ref, g_ref, b_ref, out_ref):
    y = _dot(o_ref[...], w_ref[...])
    z = DEEPNORM_ALPHA * h_ref[...] + gate_ref[0] * y
    out_ref[...] = _layer_norm(z, g_ref[...], b_ref[...])


def proj_res_ln(o, w, h, mod_tab, gate_idx, ln_g, ln_b):
    nt, k = o.shape
    d = w.shape[1]
    return pl.pallas_call(
        _proj_res_ln_kernel,
        grid=(nt // TM,),
        in_specs=[pl.BlockSpec((TM, k), lambda i: (i, 0)),
                  pl.BlockSpec((k, d), lambda i: (0, 0)),
                  pl.BlockSpec((TM, d), lambda i: (i, 0)),
                  pl.BlockSpec((1, 1, d), lambda i: (i, 0, gate_idx)),
                  pl.BlockSpec((1, d), lambda i: (0, 0)),
                  pl.BlockSpec((1, d), lambda i: (0, 0))],
        out_specs=pl.BlockSpec((TM, d), lambda i: (i, 0)),
        out_shape=jax.ShapeDtypeStruct((nt, d), F32),
        compiler_params=_cparams("parallel"),
        name="proj_res_ln",
    )(o, w, h, mod_tab, ln_g.reshape(1, d), ln_b.reshape(1, d))


def _mix_proj_res_ln_kernel(oa_ref, ob_ref, oc_ref, w_ref, h_ref, gate_ref, g_ref, b_ref, out_ref):
    s1, s2 = NA_W, NA_W + RW_W
    y = (_dot(oa_ref[...], w_ref[0:s1, :]) + _dot(ob_ref[...], w_ref[s1:s2, :])
         + _dot(oc_ref[...], w_ref[s2:, :]))
    z = DEEPNORM_ALPHA * h_ref[...] + gate_ref[0] * y
    out_ref[...] = _layer_norm(z, g_ref[...], b_ref[...])


def mix_proj_res_ln(oa, ob, oc, w, h, mod_tab, gate_idx, ln_g, ln_b):
    nt, d = h.shape
    row = lambda n: pl.BlockSpec((TM, n), lambda i: (i, 0))
    return pl.pallas_call(
        _mix_proj_res_ln_kernel,
        grid=(nt // TM,),
        in_specs=[row(NA_W), row(RW_W), row(SSM_W),
                  pl.BlockSpec((d, d), lambda i: (0, 0)),
                  row(d),
                  pl.BlockSpec((1, 1, d), lambda i: (i, 0, gate_idx)),
                  pl.BlockSpec((1, d), lambda i: (0, 0)),
                  pl.BlockSpec((1, d), lambda i: (0, 0))],
        out_specs=row(d),
        out_shape=jax.ShapeDtypeStruct((nt, d), F32),
        compiler_params=_cparams("parallel"),
        name="mix_proj_res_ln",
    )(oa, ob, oc, w, h, mod_tab, ln_g.reshape(1, d), ln_b.reshape(1, d))


def _na_bias_table(rpb, rows):
    win_r = min(NA_WIN_R, rows)
    q = np.arange(GRID_W)
    col_start = np.clip(q - NA_WIN_C // 2, 0, GRID_W - NA_WIN_C)
    kc = np.arange(GRID_W)
    valid = (kc[None, :] >= col_start[:, None]) & (kc[None, :] < col_start[:, None] + NA_WIN_C)
    crel = np.clip(kc[None, :] - q[:, None] + NA_WIN_C - 1, 0, 2 * NA_WIN_C - 2)
    case = np.arange(NA_WIN_R)
    rrel = np.clip(NA_WIN_R - 1 - case[:, None] + np.arange(win_r)[None, :], 0, 2 * NA_WIN_R - 2)
    t = rpb[:, rrel]
    t = t[:, :, :, crel]
    t = jnp.transpose(t, (1, 0, 3, 2, 4))
    t = jnp.where(valid[None, None, :, None, :], t, MASK_NEG)
    return t.reshape(NA_WIN_R, NA_HEADS * GRID_W, win_r * GRID_W).astype(F32)


NA_ROWS_PER_STEP = 4


def _attend(q, keysets, scale):
    head = lax.broadcasted_iota(jnp.int32, (GRID_W, NA_W), 1) // HEAD_DIM
    zero = jnp.zeros_like(q)
    qs = jnp.concatenate([jnp.where(head == h, q, zero) for h in range(NA_HEADS)], axis=0)
    scores = []
    for keys, _, bias in keysets:
        s = _dot_nt(qs, keys) * scale
        scores.append(s if bias is None else s + bias)
    m = scores[0].max(-1, keepdims=True)
    for s in scores[1:]:
        m = jnp.maximum(m, s.max(-1, keepdims=True))
    den = 0.0
    acc = 0.0
    for s, (_, vals, _) in zip(scores, keysets):
        p = jnp.exp(s - m)
        den = den + p.sum(-1, keepdims=True)
        acc = acc + _dot(p.astype(BF16), vals)
    acc = acc / den
    out = jnp.where(head == 0, acc[0:GRID_W], 0.0)
    for h in range(1, NA_HEADS):
        out = out + jnp.where(head == h, acc[h * GRID_W:(h + 1) * GRID_W], 0.0)
    return out


def _na_kernel(u_ref, bias_ref, o_ref, *, ctx_len, rows, scale):
    j = pl.program_id(1)
    rps = NA_ROWS_PER_STEP
    n_ctx_steps = ctx_len // (rps * GRID_W)
    win_r = min(NA_WIN_R, rows)
    kc = u_ref[0, 0:ctx_len, NA_W:2 * NA_W]
    vc = u_ref[0, 0:ctx_len, 2 * NA_W:3 * NA_W]

    @pl.when(j < n_ctx_steps)
    def _ctx():
        for i in range(rps):
            q0 = pl.multiple_of((j * rps + i) * GRID_W, GRID_W)
            q = u_ref[0, pl.ds(q0, GRID_W), 0:NA_W]
            o_ref[0, i * GRID_W:(i + 1) * GRID_W, :] = _attend(q, [(kc, vc, None)], scale).astype(o_ref.dtype)

    @pl.when(j >= n_ctx_steps)
    def _lat():
        for i in range(rps):
            r = (j - n_ctx_steps) * rps + i
            r0 = jnp.clip(r - win_r // 2, 0, rows - win_r)
            q = u_ref[0, pl.ds(pl.multiple_of(ctx_len + r * GRID_W, GRID_W), GRID_W), 0:NA_W]
            kstart = pl.multiple_of(ctx_len + r0 * GRID_W, GRID_W)
            kl = u_ref[0, pl.ds(kstart, win_r * GRID_W), NA_W:2 * NA_W]
            vl = u_ref[0, pl.ds(kstart, win_r * GRID_W), 2 * NA_W:3 * NA_W]
            out = _attend(q, [(kl, vl, bias_ref[r - r0]), (kc, vc, None)], scale)
            o_ref[0, i * GRID_W:(i + 1) * GRID_W, :] = out.astype(o_ref.dtype)


def na_attention(ua, bias_tab, ctx_len):
    bsz, tp, _ = ua.shape
    rows = (tp - ctx_len) // GRID_W
    qn = NA_ROWS_PER_STEP * GRID_W
    assert ctx_len % qn == 0 and rows % NA_ROWS_PER_STEP == 0
    return pl.pallas_call(
        functools.partial(_na_kernel, ctx_len=ctx_len, rows=rows, scale=HEAD_DIM ** -0.5),
        grid=(bsz, tp // qn),
        in_specs=[pl.BlockSpec((1, tp, NA_COLS), lambda b, j: (b, 0, 0)),
                  pl.BlockSpec(bias_tab.shape, lambda b, j: (0, 0, 0))],
        out_specs=pl.BlockSpec((1, qn, NA_W), lambda b, j: (b, j, 0)),
        out_shape=jax.ShapeDtypeStruct((bsz, tp, NA_W), BF16),
        compiler_params=_cparams("parallel", "arbitrary"),
        name="na_attention",
    )(ua, bias_tab)


RW_KQ = 4
RW_KJ = HEAD_DIM // RW_KQ
RW_TBLK = 64
HALO = 8


def _segment_edges(i, blocks_per_batch, ctx_blocks):
    j = i % blocks_per_batch
    first = jnp.logical_or(j == 0, j == ctx_blocks)
    last = jnp.logical_or(j == ctx_blocks - 1, j == blocks_per_batch - 1)
    return first, last


def _with_halo(x, prev, nxt, first, last):
    return jnp.concatenate([jnp.where(first, 0.0, prev), x, jnp.where(last, 0.0, nxt)], axis=0)


def _shifted(xe, s):
    return pltpu.roll(xe, s % xe.shape[0], 0)[HALO:HALO + TM]


def _halo_specs(nt, width):
    return [pl.BlockSpec((TM, width), lambda i: (i, 0)),
            pl.BlockSpec((HALO, width), lambda i: (jnp.maximum(i * (TM // HALO) - 1, 0), 0)),
            pl.BlockSpec((HALO, width), lambda i: (jnp.minimum((i + 1) * (TM // HALO), nt // HALO - 1), 0))]


def _softplus(x):
    return jnp.maximum(x, 0.0) + jnp.log1p(jnp.exp(-jnp.abs(x)))


def _dot_f32(a, b):
    return jnp.dot(a, b, precision=lax.Precision.HIGHEST, preferred_element_type=F32)


def _rwkv_prep_kernel(u_ref, up_ref, un_ref, mu_ref, lw_ref, g2_ref, w0_ref, a0_ref, kk_ref, ka_ref, rk_ref,
                      ones_ref, a_out, r_out, v_out, zf_out, zb_out, gb_out, *, blocks_per_batch, ctx_blocks):
    first, last = _segment_edges(pl.program_id(0), blocks_per_batch, ctx_blocks)
    xe = _with_halo(u_ref[...], up_ref[...], un_ref[...], first, last)
    x = u_ref[...]
    um = x + mu_ref[...] * (0.5 * (_shifted(xe, 1) + _shifted(xe, -1)) - x)
    w = RW_W
    r, k, v = um[:, 0:w], um[:, w:2 * w], um[:, 2 * w:3 * w]
    lo = um[:, 3 * w:3 * w + LANES]
    lane = lax.broadcasted_iota(jnp.int32, lo.shape, 1)
    lo = jnp.where(lane < 2 * RW_DECAY_LORA, jnp.tanh(lo), lo)
    lora = _dot_f32(lo, lw_ref[...])
    g = _dot_f32(jax.nn.sigmoid(um[:, 3 * w + LANES:3 * w + 2 * LANES]), g2_ref[...])
    wlog = -_softplus(-(w0_ref[...] + lora[:, :2 * w])) - 0.5
    decay = jnp.exp(-jnp.exp(wlog))
    aa = jax.nn.sigmoid(a0_ref[...] + lora[:, 2 * w:])
    ones = ones_ref[...]
    kk = k * kk_ref[...]
    kk = kk * lax.rsqrt(_dot_f32(kk * kk, ones) + 1e-12)
    ka = ka_ref[...]
    kd_f = k * (1.0 + (aa[:, :w] - 1.0) * ka)
    kd_b = k * (1.0 + (aa[:, w:] - 1.0) * ka)
    a_out[...] = -kk
    r_out[...] = r
    v_out[...] = v
    zf_out[:, 0:w] = decay[:, :w]
    zf_out[:, w:2 * w] = kk * aa[:, :w]
    zf_out[:, 2 * w:] = kd_f
    zb_out[:, 0:w] = decay[:, w:]
    zb_out[:, w:2 * w] = kk * aa[:, w:]
    zb_out[:, 2 * w:] = kd_b
    gb_out[:, 0:w] = g
    gb_out[:, w:] = _dot_f32(r * (kd_f + kd_b) * rk_ref[...], ones) * v


def rwkv_prep(ub, blocks_per_batch, ctx_blocks, mu, w0, w2, a0, a2, g2, k_k, k_a, r_k):
    nt = ub.shape[0]
    w = RW_W
    lw = jnp.zeros((LANES, 4 * w), F32)
    lw = lw.at[0:32, 0:w].set(w2[0]).at[32:64, w:2 * w].set(w2[1])
    lw = lw.at[64:96, 2 * w:3 * w].set(a2[0]).at[96:128, 3 * w:].set(a2[1])
    g2p = jnp.pad(g2, ((0, LANES - RW_GATE_LORA), (0, 0)))
    mup = jnp.pad(mu, (0, RW_PAD - RW_COLS)).reshape(1, RW_PAD)
    head = np.arange(w) // HEAD_DIM
    ones = jnp.asarray((head[:, None] == head[None, :]).astype(np.float32))
    const = lambda shape: pl.BlockSpec(shape, lambda i: (0, 0))
    row = lambda n: pl.BlockSpec((TM, n), lambda i: (i, 0))
    return pl.pallas_call(
        functools.partial(_rwkv_prep_kernel, blocks_per_batch=blocks_per_batch, ctx_blocks=ctx_blocks),
        grid=(nt // TM,),
        in_specs=_halo_specs(nt, RW_PAD) + [const((1, RW_PAD)), const((LANES, 4 * w)), const((LANES, w)),
                                            const((1, 2 * w)), const((1, 2 * w)), const((1, w)), const((1, w)),
                                            const((1, w)), const((w, w))],
        out_specs=[row(w), row(w), row(w), row(3 * w), row(3 * w), row(2 * w)],
        out_shape=[jax.ShapeDtypeStruct((nt, n), F32) for n in (w, w, w, 3 * w, 3 * w, 2 * w)],
        compiler_params=_cparams("parallel"),
        name="rwkv_prep",
    )(ub, ub, ub, mup, lw, g2p, w0.reshape(1, 2 * w), a0.reshape(1, 2 * w), k_k.reshape(1, w),
      k_a.reshape(1, w), r_k.reshape(1, w), ones)


def _rwkv_scan_kernel(af_ref, afn_ref, ab_ref, abn_ref, rf_ref, rb_ref, vf_ref, vb_ref, zf_ref, zb_ref,
                      yf_ref, yb_ref, s_ref, sa_ref, yraw_ref, *, tblk):
    @pl.when(pl.program_id(0) == 0)
    def _init():
        s_ref[...] = jnp.zeros_like(s_ref)
        sa_ref[...] = jnp.zeros_like(sa_ref)
        yraw_ref[...] = jnp.zeros_like(yraw_ref)

    quarter = LANES // RW_KQ

    def over_keys(p):
        ax = p.ndim - 1
        return (p + pltpu.roll(p, quarter, ax) + pltpu.roll(p, 2 * quarter, ax)
                + pltpu.roll(p, 3 * quarter, ax))

    def rows(ref_row):
        return jnp.broadcast_to(ref_row, (HEAD_DIM, LANES))

    def one_direction(d, tc, tn, at_end, a_ref, an_ref, r_ref, v_ref, z_ref, sa):
        v_t = v_ref[tc]
        y = jnp.zeros((HEAD_DIM, LANES), F32)
        sa_next = jnp.zeros((HEAD_DIM, LANES), F32)
        for j in range(RW_KJ):
            js = slice(j, j + 1)
            s_new = (s_ref[d, j] * rows(z_ref[tc, 0, js, :]) + sa * rows(z_ref[tc, 1, js, :])
                     + v_t * rows(z_ref[tc, 2, js, :]))
            s_ref[d, j] = s_new
            y = y + s_new * rows(r_ref[tc, js, :])
            sa_next = sa_next + s_new * rows(jnp.where(at_end, an_ref[0, js, :], a_ref[tn, js, :]))
        return sa_next, y

    lane_y = lax.broadcasted_iota(jnp.int32, (RW_KJ, LANES), 1)
    lower_half = lane_y < 2 * quarter
    even_quarter = (lane_y & quarter) == 0

    def finish_rows(y):
        b0, b1, b2, b3 = (y[q * RW_KJ:(q + 1) * RW_KJ] for q in range(RW_KQ))
        s02 = jnp.where(lower_half, b0, b2) + pltpu.roll(jnp.where(lower_half, b2, b0), 2 * quarter, 1)
        s13 = jnp.where(lower_half, b1, b3) + pltpu.roll(jnp.where(lower_half, b3, b1), 2 * quarter, 1)
        return jnp.where(even_quarter, s02 + pltpu.roll(s02, 3 * quarter, 1), s13 + pltpu.roll(s13, quarter, 1))

    def step(t, carry):
        sa_f, part_b = carry
        tb = tblk - 1 - t
        at_end = t == tblk - 1
        sa_b = over_keys(part_b)
        tb_prev = jnp.minimum(tb + 1, tblk - 1)
        yb_ref[tb_prev] = finish_rows(yraw_ref[tb_prev])
        part_f, y_f = one_direction(0, t, jnp.minimum(t + 1, tblk - 1), at_end, af_ref, afn_ref, rf_ref,
                                    vf_ref, zf_ref, sa_f)
        yf_ref[t] = finish_rows(y_f)
        part_b, y_b = one_direction(1, tb, jnp.maximum(tb - 1, 0), at_end, ab_ref, abn_ref, rb_ref, vb_ref,
                                    zb_ref, sa_b)
        yraw_ref[tb] = y_b
        return over_keys(part_f), part_b

    sa_f, part_b = lax.fori_loop(0, tblk, step, (sa_ref[0], sa_ref[1]))
    sa_ref[0] = sa_f
    sa_ref[1] = part_b
    yb_ref[0] = finish_rows(yraw_ref[0])


def rwkv_scan(a_s, r_s, v_s, zf_s, zb_s, ctx_len, tblk=RW_TBLK):
    t = a_s.shape[0]
    nblk = t // tblk
    ncb = ctx_len // tblk
    kh = RW_KJ

    def rev(i):
        return jnp.where(i < ncb, ncb - 1 - i, nblk - 1 + ncb - i)

    def fwd3(i):
        return (i, 0, 0)

    def bwd3(i):
        return (rev(i), 0, 0)

    return pl.pallas_call(
        functools.partial(_rwkv_scan_kernel, tblk=tblk),
        grid=(nblk,),
        in_specs=[pl.BlockSpec((tblk, kh, LANES), fwd3),
                  pl.BlockSpec((1, kh, LANES), lambda i: (jnp.minimum(i + 1, nblk - 1) * tblk, 0, 0)),
                  pl.BlockSpec((tblk, kh, LANES), bwd3),
                  pl.BlockSpec((1, kh, LANES), lambda i: (rev(jnp.minimum(i + 1, nblk - 1)) * tblk + tblk - 1, 0, 0)),
                  pl.BlockSpec((tblk, kh, LANES), fwd3),
                  pl.BlockSpec((tblk, kh, LANES), bwd3),
                  pl.BlockSpec((tblk, HEAD_DIM, LANES), fwd3),
                  pl.BlockSpec((tblk, HEAD_DIM, LANES), bwd3),
                  pl.BlockSpec((tblk, 3, kh, LANES), lambda i: (i, 0, 0, 0)),
                  pl.BlockSpec((tblk, 3, kh, LANES), lambda i: (rev(i), 0, 0, 0))],
        out_specs=[pl.BlockSpec((tblk, kh, LANES), fwd3), pl.BlockSpec((tblk, kh, LANES), bwd3)],
        out_shape=[jax.ShapeDtypeStruct((t, kh, LANES), F32)] * 2,
        scratch_shapes=[pltpu.VMEM((2, kh, HEAD_DIM, LANES), F32), pltpu.VMEM((2, HEAD_DIM, LANES), F32),
                        pltpu.VMEM((tblk, HEAD_DIM, LANES), F32)],
        compiler_params=_cparams("arbitrary"),
        name="rwkv_scan",
    )(a_s, a_s, a_s, a_s, r_s, r_s, v_s, v_s, zf_s, zb_s)


def _rwkv_post_kernel(yf_ref, yb_ref, gb_ref, lw_ref, lb_ref, ones_ref, o_ref):
    w = RW_W
    ones = ones_ref[...]
    y = yf_ref[...] + yb_ref[...]
    yc = y - _dot_f32(y, ones) * (1.0 / HEAD_DIM)
    var = _dot_f32(yc * yc, ones) * (1.0 / HEAD_DIM)
    yn = yc * lax.rsqrt(var + RW_GN_EPS) * lw_ref[...] + lb_ref[...]
    o_ref[...] = ((yn + gb_ref[:, w:]) * gb_ref[:, 0:w]).astype(o_ref.dtype)


def rwkv_post(yf, yb, gb, ln_w, ln_b):
    nt, w = yf.shape
    head = np.arange(w) // HEAD_DIM
    ones = jnp.asarray((head[:, None] == head[None, :]).astype(np.float32))
    row = lambda n: pl.BlockSpec((TM, n), lambda i: (i, 0))
    const = lambda shape: pl.BlockSpec(shape, lambda i: (0, 0))
    return pl.pallas_call(
        _rwkv_post_kernel,
        grid=(nt // TM,),
        in_specs=[row(w), row(w), row(2 * w), const((1, w)), const((1, w)), const((w, w))],
        out_specs=row(w),
        out_shape=jax.ShapeDtypeStruct((nt, w), BF16),
        compiler_params=_cparams("parallel"),
        name="rwkv_post",
    )(yf, yb, gb, ln_w.reshape(1, w), ln_b.reshape(1, w), ones)


def _rw_keys_to_scan(z, bsz, nq):
    t = z.shape[0] // bsz
    z = z.reshape(bsz, t, nq, RW_HEADS, RW_KQ, RW_KJ)
    z = jnp.transpose(z, (1, 2, 5, 4, 0, 3))
    return z.reshape(t, nq, RW_KJ, LANES)


def _rw_vals_to_scan(v, bsz):
    t = v.shape[0] // bsz
    z = jnp.transpose(v.reshape(bsz, t, RW_HEADS, HEAD_DIM), (1, 3, 0, 2))
    z = jnp.broadcast_to(z[:, :, None], (t, HEAD_DIM, RW_KQ, bsz, RW_HEADS))
    return z.reshape(t, HEAD_DIM, LANES)


def _rw_vals_from_scan(y, bsz):
    t = y.shape[0]
    z = y.reshape(t, RW_KJ, RW_KQ, bsz, RW_HEADS)
    z = jnp.transpose(z, (3, 0, 4, 2, 1))
    return z.reshape(bsz * t, RW_W)


def rwkv_mixer(ub, bsz, ctx_len, mu, w0, w2, a0, a2, g2, k_k, k_a, r_k, ln_w, ln_b, tblk=RW_TBLK):
    tp = ub.shape[0] // bsz
    a_t, r_t, v_t, zf_t, zb_t, gb = rwkv_prep(ub, tp // TM, ctx_len // TM, mu, w0, w2, a0, a2, g2, k_k, k_a, r_k)
    yf, yb = rwkv_scan(_rw_keys_to_scan(a_t, bsz, 1)[:, 0], _rw_keys_to_scan(r_t, bsz, 1)[:, 0],
                       _rw_vals_to_scan(v_t, bsz), _rw_keys_to_scan(zf_t, bsz, 3), _rw_keys_to_scan(zb_t, bsz, 3),
                       ctx_len, tblk)
    return rwkv_post(_rw_vals_from_scan(yf, bsz), _rw_vals_from_scan(yb, bsz), gb, ln_w, ln_b)


def _ssd_chunk(d, x_ref, bt_ref, c_ref, b_ref, dt_ref, dtt_ref, a_ref, at_ref, y_ref, st_ref):
    q = SSM_CHUNK
    hp = lax.Precision.HIGHEST
    row = lax.broadcasted_iota(jnp.int32, (q, q), 0)
    col = lax.broadcasted_iota(jnp.int32, (q, q), 1)
    mask = row >= col if d == 0 else row <= col
    tri = mask.astype(F32)
    dt = dt_ref[0, 0]
    dta = dt * a_ref[d]
    dta_t = dtt_ref[0, 0] * at_ref[d]
    acs = jnp.dot(tri, dta, precision=hp, preferred_element_type=F32)
    acs_t = lax.dot_general(dta_t, tri, (((1,), (1,)), ((), ())), precision=hp,
                            preferred_element_type=F32)
    tot = jnp.sum(dta, axis=0, keepdims=True)
    dec = jnp.exp(tot - acs)
    eacs = jnp.exp(acs)
    etot = jnp.exp(tot)
    x = x_ref[0]
    for g in range(SSM_GROUPS):
        gs = slice(g * SSM_STATE, (g + 1) * SSM_STATE)
        cm = c_ref[0, :, gs]
        bm = b_ref[0, :, gs]
        bt = bt_ref[0, gs, :]
        cb = _dot_nt(cm, bm)
        for eh in range(SSM_HPG):
            e = g * SSM_HPG + eh
            es = slice(e * HEAD_DIM, (e + 1) * HEAD_DIM)
            seg = jnp.exp(jnp.where(mask, acs[:, e:e + 1] - acs_t[e:e + 1, :], MASK_NEG))
            xdt = x[:, es] * dt[:, e:e + 1]
            y_diag = _dot((cb * seg).astype(BF16), xdt.astype(BF16))
            state = st_ref[d, e]
            y_off = _dot(cm, state.astype(BF16)) * eacs[:, e:e + 1]
            y_ref[0, :, es] = y_diag + y_off
            contrib = _dot(bt, (xdt * dec[:, e:e + 1]).astype(BF16))
            st_ref[d, e] = state * etot[:, e:e + 1] + contrib


def _ssd_kernel(xf_ref, btf_ref, cf_ref, bf_ref, dtf_ref, dttf_ref,
                xb_ref, btb_ref, cb_ref, bb_ref, dtb_ref, dttb_ref, a_ref, at_ref, yf_ref, yb_ref, st_ref):
    @pl.when(pl.program_id(1) == 0)
    def _init():
        st_ref[...] = jnp.zeros_like(st_ref)

    _ssd_chunk(0, xf_ref, btf_ref, cf_ref, bf_ref, dtf_ref, dttf_ref, a_ref, at_ref, yf_ref, st_ref)
    _ssd_chunk(1, xb_ref, btb_ref, cb_ref, bb_ref, dtb_ref, dttb_ref, a_ref, at_ref, yb_ref, st_ref)


def ssd_scan(xs, bm, cm, dt, a, ctx_len):
    bsz, t, _ = xs.shape
    q = SSM_CHUNK
    nc = t // q
    nctx = ctx_len // q
    bt = jnp.swapaxes(bm, 1, 2)
    dtt = jnp.swapaxes(dt, 2, 3)

    def cidx(d, j):
        return j if d == 0 else jnp.where(j < nctx, nctx - 1 - j, nc - 1 - (j - nctx))

    def specs(d):
        return [pl.BlockSpec((1, q, SSM_W), lambda b, j: (b, cidx(d, j), 0)),
                pl.BlockSpec((1, SSM_GN, q), lambda b, j: (b, 0, cidx(d, j))),
                pl.BlockSpec((1, q, SSM_GN), lambda b, j: (b, cidx(d, j), 0)),
                pl.BlockSpec((1, q, SSM_GN), lambda b, j: (b, cidx(d, j), 0)),
                pl.BlockSpec((1, 1, q, SSM_HEADS), lambda b, j: (b, d, cidx(d, j), 0)),
                pl.BlockSpec((1, 1, SSM_HEADS, q), lambda b, j: (b, d, 0, cidx(d, j)))]

    dir_args = (xs, bt, cm, bm, dt, dtt)
    return pl.pallas_call(
        _ssd_kernel,
        grid=(bsz, nc),
        in_specs=specs(0) + specs(1) + [pl.BlockSpec((2, 1, SSM_HEADS), lambda b, j: (0, 0, 0)),
                                        pl.BlockSpec((2, SSM_HEADS, 1), lambda b, j: (0, 0, 0))],
        out_specs=[pl.BlockSpec((1, q, SSM_W), lambda b, j: (b, cidx(0, j), 0)),
                   pl.BlockSpec((1, q, SSM_W), lambda b, j: (b, cidx(1, j), 0))],
        out_shape=[jax.ShapeDtypeStruct((bsz, t, SSM_W), F32)] * 2,
        scratch_shapes=[pltpu.VMEM((2, SSM_HEADS, SSM_STATE, HEAD_DIM), F32)],
        compiler_params=_cparams("parallel", "arbitrary"),
        name="ssd_scan",
    )(*dir_args, *dir_args, a.reshape(2, 1, SSM_HEADS), a.reshape(2, SSM_HEADS, 1))


def _ssm_prep_kernel(u_ref, up_ref, un_ref, cw_ref, cb_ref, dtb_ref, xs_out, b_out, c_out, dt_out,
                     *, blocks_per_batch, ctx_blocks):
    first, last = _segment_edges(pl.program_id(0), blocks_per_batch, ctx_blocks)
    lo, hi = SSM_W, SSM_W + SSM_XBC
    xe = _with_halo(u_ref[:, lo:hi], up_ref[:, lo:hi], un_ref[:, lo:hi], first, last)
    acc = cb_ref[...] + cw_ref[0:1, :] * _shifted(xe, SSM_CONV // 2)
    for i in range(1, SSM_CONV):
        acc = acc + cw_ref[i:i + 1, :] * _shifted(xe, SSM_CONV // 2 - i)
    xbc = acc * jax.nn.sigmoid(acc)
    xs_out[...] = xbc[:, :SSM_W]
    b_out[...] = xbc[:, SSM_W:SSM_W + SSM_GN].astype(b_out.dtype)
    c_out[...] = xbc[:, SSM_W + SSM_GN:].astype(c_out.dtype)
    dt_out[...] = _softplus(u_ref[:, hi:] + dtb_ref[...])


def ssm_prep(uc, blocks_per_batch, ctx_blocks, conv_w, conv_b, dt_bias):
    nt = uc.shape[0]
    npad = SSM_PAD - SSM_W - SSM_XBC
    dtb = jnp.pad(dt_bias.reshape(-1), (0, npad - 2 * SSM_HEADS)).reshape(1, npad)
    const = lambda shape: pl.BlockSpec(shape, lambda i: (0, 0))
    row = lambda n: pl.BlockSpec((TM, n), lambda i: (i, 0))
    return pl.pallas_call(
        functools.partial(_ssm_prep_kernel, blocks_per_batch=blocks_per_batch, ctx_blocks=ctx_blocks),
        grid=(nt // TM,),
        in_specs=_halo_specs(nt, SSM_PAD) + [const((SSM_CONV, SSM_XBC)), const((1, SSM_XBC)), const((1, npad))],
        out_specs=[row(SSM_W), row(SSM_GN), row(SSM_GN), row(npad)],
        out_shape=[jax.ShapeDtypeStruct((nt, SSM_W), F32), jax.ShapeDtypeStruct((nt, SSM_GN), BF16),
                   jax.ShapeDtypeStruct((nt, SSM_GN), BF16), jax.ShapeDtypeStruct((nt, npad), F32)],
        compiler_params=_cparams("parallel"),
        name="ssm_prep",
    )(uc, uc, uc, conv_w, conv_b.reshape(1, SSM_XBC), dtb)


def _ssm_post_kernel(yf_ref, yb_ref, xs_ref, z_ref, d_ref, nw_ref, o_ref):
    z = z_ref[...]
    y = (yf_ref[...] + yb_ref[...] + d_ref[...] * xs_ref[...]) * (z * jax.nn.sigmoid(z))
    gw = SSM_HPG * HEAD_DIM
    for g in range(SSM_GROUPS):
        gs = slice(g * gw, (g + 1) * gw)
        yg = y[:, gs]
        ms = jnp.mean(yg * yg, -1, keepdims=True)
        o_ref[:, gs] = (yg * lax.rsqrt(ms + RMS_EPS) * nw_ref[:, gs]).astype(o_ref.dtype)


def ssm_post(yf, yb, xs, uc, d_skip, norm_w):
    nt = xs.shape[0]
    row = pl.BlockSpec((TM, SSM_W), lambda i: (i, 0))
    const = pl.BlockSpec((1, SSM_W), lambda i: (0, 0))
    return pl.pallas_call(
        _ssm_post_kernel,
        grid=(nt // TM,),
        in_specs=[row, row, row, row, const, const],
        out_specs=row,
        out_shape=jax.ShapeDtypeStruct((nt, SSM_W), BF16),
        compiler_params=_cparams("parallel"),
        name="ssm_post",
    )(yf, yb, xs, uc, jnp.repeat(d_skip, HEAD_DIM).reshape(1, SSM_W), norm_w.reshape(1, SSM_W))


def ssm_mixer(uc, bsz, ctx_len, conv_w, conv_b, dt_bias, a_log, d_skip, norm_w):
    nt = uc.shape[0]
    tp = nt // bsz
    xs, bm, cm, dtp = ssm_prep(uc, tp // TM, ctx_len // TM, conv_w, conv_b, dt_bias)
    dt = jnp.transpose(dtp[:, :2 * SSM_HEADS].reshape(bsz, tp, 2, SSM_HEADS), (0, 2, 1, 3))
    a = -jnp.exp(a_log.astype(F32)).reshape(2, SSM_HEADS)
    yf, yb = ssd_scan(xs.reshape(bsz, tp, SSM_W), bm.reshape(bsz, tp, SSM_GN), cm.reshape(bsz, tp, SSM_GN),
                      dt, a, ctx_len)
    return ssm_post(yf.reshape(nt, SSM_W), yb.reshape(nt, SSM_W), xs, uc, d_skip, norm_w)


def _adaln_router_kernel(x_ref, sh_ref, sc_ref, wr_ref, f_ref, lg_ref):
    xm = x_ref[...] * (1.0 + sc_ref[0]) + sh_ref[0]
    f_ref[...] = xm.astype(f_ref.dtype)
    lg_ref[...] = jnp.dot(xm, wr_ref[...], precision=lax.Precision.HIGHEST, preferred_element_type=F32)


def adaln_router(x, mod_tab, sh_idx, sc_idx, wr):
    nt, d = x.shape
    n = wr.shape[1]
    return pl.pallas_call(
        _adaln_router_kernel,
        grid=(nt // TM,),
        in_specs=[pl.BlockSpec((TM, d), lambda i: (i, 0))] + _mod_specs(d, sh_idx, sc_idx)
        + [pl.BlockSpec((d, n), lambda i: (0, 0))],
        out_specs=[pl.BlockSpec((TM, d), lambda i: (i, 0)), pl.BlockSpec((TM, n), lambda i: (i, 0))],
        out_shape=[jax.ShapeDtypeStruct((nt, d), BF16), jax.ShapeDtypeStruct((nt, n), F32)],
        compiler_params=_cparams("parallel"),
        name="adaln_router",
    )(x, mod_tab, mod_tab, wr)


def _expert_changed(be_ref, i):
    return jnp.logical_or(i == 0, be_ref[i] != be_ref[jnp.maximum(i - 1, 0)])


def _moe_up_kernel(be_ref, x_ref, wg_ref, wu_ref, o_ref, wg_bf, wu_bf):
    @pl.when(_expert_changed(be_ref, pl.program_id(1)))
    def _cast():
        wg_bf[...] = wg_ref[0].astype(BF16)
        wu_bf[...] = wu_ref[0].astype(BF16)

    x = x_ref[...]
    g = _dot(x, wg_bf[...])
    u = _dot(x, wu_bf[...])
    o_ref[...] = (g * jax.nn.sigmoid(g) * u).astype(o_ref.dtype)


def moe_up(xb, block_e, wg, wu, n_split=2):
    ns, d = xb.shape
    f = wg.shape[2]
    tn = f // n_split
    return pl.pallas_call(
        _moe_up_kernel,
        grid_spec=pltpu.PrefetchScalarGridSpec(
            num_scalar_prefetch=1,
            grid=(n_split, ns // MOE_BLOCK),
            in_specs=[pl.BlockSpec((MOE_BLOCK, d), lambda n, i, be: (i, 0)),
                      pl.BlockSpec((1, d, tn), lambda n, i, be: (be[i], 0, n)),
                      pl.BlockSpec((1, d, tn), lambda n, i, be: (be[i], 0, n))],
            out_specs=pl.BlockSpec((MOE_BLOCK, tn), lambda n, i, be: (i, n)),
            scratch_shapes=[pltpu.VMEM((d, tn), BF16), pltpu.VMEM((d, tn), BF16)]),
        out_shape=jax.ShapeDtypeStruct((ns, f), BF16),
        compiler_params=_cparams("arbitrary", "arbitrary"),
        name="moe_up",
    )(block_e, xb, wg, wu)


def _moe_down_kernel(be_ref, a_ref, wd_ref, o_ref, wd_bf):
    @pl.when(_expert_changed(be_ref, pl.program_id(0)))
    def _cast():
        wd_bf[...] = wd_ref[0].astype(BF16)

    o_ref[...] = _dot(a_ref[...], wd_bf[...])


def moe_down(act, block_e, wd):
    ns, f = act.shape
    d = wd.shape[2]
    return pl.pallas_call(
        _moe_down_kernel,
        grid_spec=pltpu.PrefetchScalarGridSpec(
            num_scalar_prefetch=1,
            grid=(ns // MOE_BLOCK,),
            in_specs=[pl.BlockSpec((MOE_BLOCK, f), lambda i, be: (i, 0)),
                      pl.BlockSpec((1, f, d), lambda i, be: (be[i], 0, 0))],
            out_specs=pl.BlockSpec((MOE_BLOCK, d), lambda i, be: (i, 0)),
            scratch_shapes=[pltpu.VMEM((f, d), BF16)]),
        out_shape=jax.ShapeDtypeStruct((ns, d), F32),
        compiler_params=_cparams("arbitrary"),
        name="moe_down",
    )(block_e, act, wd)


def _moe_res_ln_kernel(h_ref, y0_ref, y1_ref, gt_ref, gate_ref, g_ref, b_ref, out_ref):
    gt = gt_ref[...]
    y = y0_ref[...] * gt[:, 0:1] + y1_ref[...] * gt[:, 1:2]
    z = DEEPNORM_ALPHA * h_ref[...] + gate_ref[0] * y
    out_ref[...] = _layer_norm(z, g_ref[...], b_ref[...])


def moe_res_ln(h, y0, y1, gates, mod_tab, gate_idx, ln_g, ln_b):
    nt, d = h.shape
    row = pl.BlockSpec((TM, d), lambda i: (i, 0))
    return pl.pallas_call(
        _moe_res_ln_kernel,
        grid=(nt // TM,),
        in_specs=[row, row, row,
                  pl.BlockSpec((TM, TOP_K), lambda i: (i, 0)),
                  pl.BlockSpec((1, 1, d), lambda i: (i, 0, gate_idx)),
                  pl.BlockSpec((1, d), lambda i: (0, 0)),
                  pl.BlockSpec((1, d), lambda i: (0, 0))],
        out_specs=row,
        out_shape=jax.ShapeDtypeStruct((nt, d), F32),
        compiler_params=_cparams("parallel"),
        name="moe_res_ln",
    )(h, y0, y1, gates, mod_tab, ln_g.reshape(1, d), ln_b.reshape(1, d))


def moe_mixer(h, mod_tab, router_w, router_b, wg, wu, wd, ln_g, ln_b):
    n_tok, d = h.shape
    wr = jnp.pad(router_w, ((0, 0), (0, ROUTER_PAD - N_EXPERTS)))
    f_bf, logits = adaln_router(h, mod_tab, 3, 4, wr)
    logits = logits[:, :N_EXPERTS] + router_b
    top_val, top_idx = lax.top_k(logits, TOP_K)
    gates = jax.nn.softmax(top_val, axis=-1)
    n_assign = n_tok * TOP_K
    flat_e = top_idx.reshape(-1)
    onehot = (flat_e[:, None] == jnp.arange(N_EXPERTS)[None, :]).astype(jnp.int32)
    csum = jnp.cumsum(onehot, axis=0)
    counts = csum[-1]
    rank = jnp.sum((csum - onehot) * onehot, axis=1)
    padded = (counts + MOE_BLOCK - 1) // MOE_BLOCK * MOE_BLOCK
    pad_end = jnp.cumsum(padded)
    pad_start = pad_end - padded
    dest = pad_start[flat_e] + rank
    n_blocks = -(-n_assign // MOE_BLOCK) + N_EXPERTS
    flat_tok = jnp.repeat(jnp.arange(n_tok, dtype=jnp.int32), TOP_K)
    slot_tok = jnp.full((n_blocks * MOE_BLOCK,), n_tok, jnp.int32).at[dest].set(flat_tok)
    block_e = jnp.minimum(jnp.searchsorted(pad_end, jnp.arange(n_blocks) * MOE_BLOCK, side='right'),
                          N_EXPERTS - 1).astype(jnp.int32)
    f_pad = jnp.concatenate([f_bf, jnp.zeros((1, d), f_bf.dtype)], axis=0)
    xb = f_pad[slot_tok]
    act = moe_up(xb, block_e, wg, wu)
    yb = moe_down(act, block_e, wd)
    dest2 = dest.reshape(n_tok, TOP_K)
    return moe_res_ln(h, yb[dest2[:, 0]], yb[dest2[:, 1]], gates, mod_tab, 5, ln_g, ln_b)


def kernel(x, c, ctx, c_ctx, w_mod, b_mod, w_in, w_out, ln1_g, ln1_b, ln2_g, ln2_b, na_rpb, rw_mu, rw_w0, rw_w2, rw_a0, rw_a2, rw_g2, rw_k_k, rw_k_a, rw_r_k, rw_ln_w, rw_ln_b, ssm_conv_w, ssm_conv_b, ssm_dt_bias, ssm_a_log, ssm_d, ssm_norm_w, ffn_w_gate, ffn_w_up, ffn_w_down, moe_router, moe_router_b, moe_w_gate, moe_w_up, moe_w_down):
    bsz, seq, d = x.shape
    ctx_len = ctx.shape[1]
    tp = ctx_len + seq
    nt = bsz * tp
    assert ctx_len % TM == 0 and seq % TM == 0 and RW_KQ * bsz * RW_HEADS == LANES
    rows = seq // GRID_W
    depth = w_mod.shape[0]

    h = jnp.concatenate([ctx, x], axis=1).reshape(nt, d)

    n_mod_rows = -(-(bsz + 1) // 16) * 16
    cin = jnp.concatenate([c, c_ctx[None, :], jnp.zeros((n_mod_rows - bsz - 1, d), F32)], axis=0)
    mod_all = modulation(cin, w_mod, b_mod)
    blk = np.arange(nt // TM)
    blk_row = np.where(blk % (tp // TM) < ctx_len // TM, bsz, blk // (tp // TM))

    for l in range(depth):
        mod_tab = mod_all[l][blk_row][:, None, :]
        wl = w_in[l]
        w_in_p = jnp.concatenate(
            [wl[:, :NA_COLS],
             jnp.pad(wl[:, NA_COLS:NA_COLS + RW_COLS], ((0, 0), (0, RW_PAD - RW_COLS))),
             jnp.pad(wl[:, NA_COLS + RW_COLS:], ((0, 0), (0, SSM_PAD - SSM_COLS)))], axis=1).astype(BF16)
        ua, ub, uc = adaln_proj(h, mod_tab, 0, 1, w_in_p, (NA_COLS, RW_PAD, SSM_PAD), (BF16, F32, F32))
        oa = na_attention(ua.reshape(bsz, tp, NA_COLS), _na_bias_table(na_rpb[l], rows), ctx_len)
        ob = rwkv_mixer(ub, bsz, ctx_len, rw_mu[l], rw_w0[l], rw_w2[l], rw_a0[l], rw_a2[l], rw_g2[l],
                        rw_k_k[l], rw_k_a[l], rw_r_k[l], rw_ln_w[l], rw_ln_b[l])
        oc = ssm_mixer(uc, bsz, ctx_len, ssm_conv_w[l], ssm_conv_b[l], ssm_dt_bias[l], ssm_a_log[l],
                       ssm_d[l], ssm_norm_w[l])
        h = mix_proj_res_ln(oa.reshape(nt, NA_W), ob, oc, w_out[l].astype(BF16), h, mod_tab, 2,
                            ln1_g[l], ln1_b[l])

        j = l // 2
        if l % 2 == 0:
            act = adaln_swiglu(h, mod_tab, 3, 4, ffn_w_gate[j].astype(BF16), ffn_w_up[j].astype(BF16))
            h = proj_res_ln(act, ffn_w_down[j].astype(BF16), h, mod_tab, 5, ln2_g[l], ln2_b[l])
        else:
            h = moe_mixer(h, mod_tab, moe_router[j], moe_router_b[j], moe_w_gate[j], moe_w_up[j],
                          moe_w_down[j], ln2_g[l], ln2_b[l])
    return h.reshape(bsz, tp, d)[:, ctx_len:]
```

```python
import functools

import numpy as np
import jax
import jax.numpy as jnp
from jax import lax
from jax.experimental import pallas as pl
from jax.experimental.pallas import tpu as pltpu

F32 = jnp.float32
BF16 = jnp.bfloat16

D_MODEL = 1024
DEPTH = 4
GRID_W = 64
HEAD_DIM = 64
NA_HEADS = 4
RW_HEADS = 4
SSM_HEADS = 8
NA_W = NA_HEADS * HEAD_DIM
RW_W = RW_HEADS * HEAD_DIM
SSM_W = SSM_HEADS * HEAD_DIM
NA_WIN_R = 8
NA_WIN_C = 16
RW_DECAY_LORA = 32
RW_AAA_LORA = 32
RW_GATE_LORA = 64
RW_GN_EPS = 64e-5
SSM_GROUPS = 2
SSM_HPG = SSM_HEADS // SSM_GROUPS
SSM_STATE = 128
SSM_CONV = 5
SSM_CHUNK = 128
SSM_GN = SSM_GROUPS * SSM_STATE
SSM_XBC = SSM_W + 2 * SSM_GN
NA_COLS = 3 * NA_W
RW_COLS = 3 * RW_W + 2 * RW_DECAY_LORA + 2 * RW_AAA_LORA + RW_GATE_LORA
SSM_COLS = SSM_W + SSM_XBC + 2 * SSM_HEADS
N_EXPERTS = 8
TOP_K = 2
MOE_BLOCK = 256
DEEPNORM_ALPHA = (2 * DEPTH) ** 0.25
LN_EPS = 1e-5
RMS_EPS = 1e-5

LANES = 128
TM = 256
RW_PAD = 1024
SSM_PAD = 1664
ROUTER_PAD = LANES
VMEM_LIMIT = 56 * 1024 * 1024
MASK_NEG = -1e30


def _cparams(*sem):
    return pltpu.CompilerParams(dimension_semantics=sem, vmem_limit_bytes=VMEM_LIMIT)


def _dot(a, b):
    return jnp.dot(a, b, preferred_element_type=F32)


def _dot_nt(a, b):
    return lax.dot_general(a, b, (((1,), (1,)), ((), ())), preferred_element_type=F32)


def _layer_norm(z, g, b):
    mu = jnp.mean(z, -1, keepdims=True)
    zc = z - mu
    var = jnp.mean(zc * zc, -1, keepdims=True)
    return zc * lax.rsqrt(var + LN_EPS) * g + b


def _mod_kernel(c_ref, w_ref, b_ref, o_ref):
    cs = c_ref[...]
    cs = cs * jax.nn.sigmoid(cs)
    o_ref[0] = _dot(cs.astype(BF16), w_ref[0].astype(BF16)) + b_ref[0]


def modulation(cin, w_mod, b_mod):
    r, d = cin.shape
    nl, _, n = w_mod.shape
    tn = d
    return pl.pallas_call(
        _mod_kernel,
        grid=(nl, n // tn),
        in_specs=[pl.BlockSpec((r, d), lambda l, j: (0, 0)),
                  pl.BlockSpec((1, d, tn), lambda l, j: (l, 0, j)),
                  pl.BlockSpec((1, 1, tn), lambda l, j: (l, 0, j))],
        out_specs=pl.BlockSpec((1, r, tn), lambda l, j: (l, 0, j)),
        out_shape=jax.ShapeDtypeStruct((nl, r, n), F32),
        compiler_params=_cparams("parallel", "parallel"),
        name="modulation",
    )(cin, w_mod, b_mod.reshape(nl, 1, n))


def _mod_specs(d, idx_a, idx_b):
    return [pl.BlockSpec((1, 1, d), lambda i: (i, 0, idx_a)),
            pl.BlockSpec((1, 1, d), lambda i: (i, 0, idx_b))]


def _adaln_proj_kernel(x_ref, sh_ref, sc_ref, w_ref, *o_refs, splits):
    xm = x_ref[...] * (1.0 + sc_ref[0]) + sh_ref[0]
    y = _dot(xm.astype(BF16), w_ref[...])
    off = 0
    for o_ref, n in zip(o_refs, splits):
        o_ref[...] = y[:, off:off + n].astype(o_ref.dtype)
        off += n


def adaln_proj(x, mod_tab, sh_idx, sc_idx, w, splits, dtypes):
    nt, d = x.shape
    n = w.shape[1]
    return pl.pallas_call(
        functools.partial(_adaln_proj_kernel, splits=splits),
        grid=(nt // TM,),
        in_specs=[pl.BlockSpec((TM, d), lambda i: (i, 0))] + _mod_specs(d, sh_idx, sc_idx)
        + [pl.BlockSpec((d, n), lambda i: (0, 0))],
        out_specs=[pl.BlockSpec((TM, s), lambda i: (i, 0)) for s in splits],
        out_shape=[jax.ShapeDtypeStruct((nt, s), dt) for s, dt in zip(splits, dtypes)],
        compiler_params=_cparams("parallel"),
        name="adaln_proj",
    )(x, mod_tab, mod_tab, w)


def _adaln_swiglu_kernel(x_ref, sh_ref, sc_ref, wg_ref, wu_ref, o_ref):
    xm = (x_ref[...] * (1.0 + sc_ref[0]) + sh_ref[0]).astype(BF16)
    g = _dot(xm, wg_ref[...])
    u = _dot(xm, wu_ref[...])
    o_ref[...] = (g * jax.nn.sigmoid(g) * u).astype(o_ref.dtype)


def adaln_swiglu(x, mod_tab, sh_idx, sc_idx, wg, wu):
    nt, d = x.shape
    f = wg.shape[1]
    return pl.pallas_call(
        _adaln_swiglu_kernel,
        grid=(nt // TM,),
        in_specs=[pl.BlockSpec((TM, d), lambda i: (i, 0))] + _mod_specs(d, sh_idx, sc_idx)
        + [pl.BlockSpec((d, f), lambda i: (0, 0)), pl.BlockSpec((d, f), lambda i: (0, 0))],
        out_specs=pl.BlockSpec((TM, f), lambda i: (i, 0)),
        out_shape=jax.ShapeDtypeStruct((nt, f), BF16),
        compiler_params=_cparams("parallel"),
        name="adaln_swiglu",
    )(x, mod_tab, mod_tab, wg, wu)


def _proj_res_ln_kernel(o_ref, w_ref, h_ref, gate_ref, g_ref, b_ref, out_ref):
    y = _dot(o_ref[...], w_ref[...])
    z = DEEPNORM_ALPHA * h_ref[...] + gate_ref[0] * y
    out_ref[...] = _layer_norm(z, g_ref[...], b_ref[...])


def proj_res_ln(o, w, h, mod_tab, gate_idx, ln_g, ln_b):
    nt, k = o.shape
    d = w.shape[1]
    return pl.pallas_call(
        _proj_res_ln_kernel,
        grid=(nt // TM,),
        in_specs=[pl.BlockSpec((TM, k), lambda i: (i, 0)),
                  pl.BlockSpec((k, d), lambda i: (0, 0)),
                  pl.BlockSpec((TM, d), lambda i: (i, 0)),
                  pl.BlockSpec((1, 1, d), lambda i: (i, 0, gate_idx)),
                  pl.BlockSpec((1, d), lambda i: (0, 0)),
                  pl.BlockSpec((1, d), lambda i: (0, 0))],
        out_specs=pl.BlockSpec((TM, d), lambda i: (i, 0)),
        out_shape=jax.ShapeDtypeStruct((nt, d), F32),
        compiler_params=_cparams("parallel"),
        name="proj_res_ln",
    )(o, w, h, mod_tab, ln_g.reshape(1, d), ln_b.reshape(1, d))


def _mix_proj_res_ln_kernel(oa_ref, ob_ref, oc_ref, w_ref, h_ref, gate_ref, g_ref, b_ref, out_ref):
    s1, s2 = NA_W, NA_W + RW_W
    y = (_dot(oa_ref[...], w_ref[0:s1, :]) + _dot(ob_ref[...], w_ref[s1:s2, :])
         + _dot(oc_ref[...], w_ref[s2:, :]))
    z = DEEPNORM_ALPHA * h_ref[...] + gate_ref[0] * y
    out_ref[...] = _layer_norm(z, g_ref[...], b_ref[...])


def mix_proj_res_ln(oa, ob, oc, w, h, mod_tab, gate_idx, ln_g, ln_b):
    nt, d = h.shape
    row = lambda n: pl.BlockSpec((TM, n), lambda i: (i, 0))
    return pl.pallas_call(
        _mix_proj_res_ln_kernel,
        grid=(nt // TM,),
        in_specs=[row(NA_W), row(RW_W), row(SSM_W),
                  pl.BlockSpec((d, d), lambda i: (0, 0)),
                  row(d),
                  pl.BlockSpec((1, 1, d), lambda i: (i, 0, gate_idx)),
                  pl.BlockSpec((1, d), lambda i: (0, 0)),
                  pl.BlockSpec((1, d), lambda i: (0, 0))],
        out_specs=row(d),
        out_shape=jax.ShapeDtypeStruct((nt, d), F32),
        compiler_params=_cparams("parallel"),
        name="mix_proj_res_ln",
    )(oa, ob, oc, w, h, mod_tab, ln_g.reshape(1, d), ln_b.reshape(1, d))


def _na_bias_table(rpb, rows):
    win_r = min(NA_WIN_R, rows)
    q = np.arange(GRID_W)
    col_start = np.clip(q - NA_WIN_C // 2, 0, GRID_W - NA_WIN_C)
    kc = np.arange(GRID_W)
    valid = (kc[None, :] >= col_start[:, None]) & (kc[None, :] < col_start[:, None] + NA_WIN_C)
    crel = np.clip(kc[None, :] - q[:, None] + NA_WIN_C - 1, 0, 2 * NA_WIN_C - 2)
    case = np.arange(NA_WIN_R)
    rrel = np.clip(NA_WIN_R - 1 - case[:, None] + np.arange(win_r)[None, :], 0, 2 * NA_WIN_R - 2)
    t = rpb[:, rrel]
    t = t[:, :, :, crel]
    t = jnp.transpose(t, (1, 0, 3, 2, 4))
    t = jnp.where(valid[None, None, :, None, :], t, MASK_NEG)
    return t.reshape(NA_WIN_R, NA_HEADS * GRID_W, win_r * GRID_W).astype(F32)


NA_ROWS_PER_STEP = 4


def _attend(q, keysets, scale):
    head = lax.broadcasted_iota(jnp.int32, (GRID_W, NA_W), 1) // HEAD_DIM
    zero = jnp.zeros_like(q)
    qs = jnp.concatenate([jnp.where(head == h, q, zero) for h in range(NA_HEADS)], axis=0)
    scores = []
    for keys, _, bias in keysets:
        s = _dot_nt(qs, keys) * scale
        scores.append(s if bias is None else s + bias)
    m = scores[0].max(-1, keepdims=True)
    for s in scores[1:]:
        m = jnp.maximum(m, s.max(-1, keepdims=True))
    den = 0.0
    acc = 0.0
    for s, (_, vals, _) in zip(scores, keysets):
        p = jnp.exp(s - m)
        den = den + p.sum(-1, keepdims=True)
        acc = acc + _dot(p.astype(BF16), vals)
    acc = acc / den
    out = jnp.where(head == 0, acc[0:GRID_W], 0.0)
    for h in range(1, NA_HEADS):
        out = out + jnp.where(head == h, acc[h * GRID_W:(h + 1) * GRID_W], 0.0)
    return out


def _na_kernel(u_ref, bias_ref, o_ref, *, ctx_len, rows, scale):
    j = pl.program_id(1)
    rps = NA_ROWS_PER_STEP
    n_ctx_steps = ctx_len // (rps * GRID_W)
    win_r = min(NA_WIN_R, rows)
    kc = u_ref[0, 0:ctx_len, NA_W:2 * NA_W]
    vc = u_ref[0, 0:ctx_len, 2 * NA_W:3 * NA_W]

    @pl.when(j < n_ctx_steps)
    def _ctx():
        for i in range(rps):
            q0 = pl.multiple_of((j * rps + i) * GRID_W, GRID_W)
            q = u_ref[0, pl.ds(q0, GRID_W), 0:NA_W]
            o_ref[0, i * GRID_W:(i + 1) * GRID_W, :] = _attend(q, [(kc, vc, None)], scale).astype(o_ref.dtype)

    @pl.when(j >= n_ctx_steps)
    def _lat():
        for i in range(rps):
            r = (j - n_ctx_steps) * rps + i
            r0 = jnp.clip(r - win_r // 2, 0, rows - win_r)
            q = u_ref[0, pl.ds(pl.multiple_of(ctx_len + r * GRID_W, GRID_W), GRID_W), 0:NA_W]
            kstart = pl.multiple_of(ctx_len + r0 * GRID_W, GRID_W)
            kl = u_ref[0, pl.ds(kstart, win_r * GRID_W), NA_W:2 * NA_W]
            vl = u_ref[0, pl.ds(kstart, win_r * GRID_W), 2 * NA_W:3 * NA_W]
            out = _attend(q, [(kl, vl, bias_ref[r - r0]), (kc, vc, None)], scale)
            o_ref[0, i * GRID_W:(i + 1) * GRID_W, :] = out.astype(o_ref.dtype)


def na_attention(ua, bias_tab, ctx_len):
    bsz, tp, _ = ua.shape
    rows = (tp - ctx_len) // GRID_W
    qn = NA_ROWS_PER_STEP * GRID_W
    assert ctx_len % qn == 0 and rows % NA_ROWS_PER_STEP == 0
    return pl.pallas_call(
        functools.partial(_na_kernel, ctx_len=ctx_len, rows=rows, scale=HEAD_DIM ** -0.5),
        grid=(bsz, tp // qn),
        in_specs=[pl.BlockSpec((1, tp, NA_COLS), lambda b, j: (b, 0, 0)),
                  pl.BlockSpec(bias_tab.shape, lambda b, j: (0, 0, 0))],
        out_specs=pl.BlockSpec((1, qn, NA_W), lambda b, j: (b, j, 0)),
        out_shape=jax.ShapeDtypeStruct((bsz, tp, NA_W), BF16),
        compiler_params=_cparams("parallel", "arbitrary"),
        name="na_attention",
    )(ua, bias_tab)


RW_KQ = 4
RW_KJ = HEAD_DIM // RW_KQ
RW_TBLK = 64
HALO = 8


def _segment_edges(i, blocks_per_batch, ctx_blocks):
    j = i % blocks_per_batch
    first = jnp.logical_or(j == 0, j == ctx_blocks)
    last = jnp.logical_or(j == ctx_blocks - 1, j == blocks_per_batch - 1)
    return first, last


def _with_halo(x, prev, nxt, first, last):
    return jnp.concatenate([jnp.where(first, 0.0, prev), x, jnp.where(last, 0.0, nxt)], axis=0)


def _shifted(xe, s):
    return pltpu.roll(xe, s % xe.shape[0], 0)[HALO:HALO + TM]


def _halo_specs(nt, width):
    return [pl.BlockSpec((TM, width), lambda i: (i, 0)),
            pl.BlockSpec((HALO, width), lambda i: (jnp.maximum(i * (TM // HALO) - 1, 0), 0)),
            pl.BlockSpec((HALO, width), lambda i: (jnp.minimum((i + 1) * (TM // HALO), nt // HALO - 1), 0))]


def _softplus(x):
    return jnp.maximum(x, 0.0) + jnp.log1p(jnp.exp(-jnp.abs(x)))


def _dot_f32(a, b):
    return jnp.dot(a, b, precision=lax.Precision.HIGHEST, preferred_element_type=F32)


def _rwkv_prep_kernel(u_ref, up_ref, un_ref, mu_ref, lw_ref, g2_ref, w0_ref, a0_ref, kk_ref, ka_ref, rk_ref,
                      ones_ref, a_out, r_out, v_out, zf_out, zb_out, gb_out, *, blocks_per_batch, ctx_blocks):
    first, last = _segment_edges(pl.program_id(0), blocks_per_batch, ctx_blocks)
    xe = _with_halo(u_ref[...], up_ref[...], un_ref[...], first, last)
    x = u_ref[...]
    um = x + mu_ref[...] * (0.5 * (_shifted(xe, 1) + _shifted(xe, -1)) - x)
    w = RW_W
    r, k, v = um[:, 0:w], um[:, w:2 * w], um[:, 2 * w:3 * w]
    lo = um[:, 3 * w:3 * w + LANES]
    lane = lax.broadcasted_iota(jnp.int32, lo.shape, 1)
    lo = jnp.where(lane < 2 * RW_DECAY_LORA, jnp.tanh(lo), lo)
    lora = _dot_f32(lo, lw_ref[...])
    g = _dot_f32(jax.nn.sigmoid(um[:, 3 * w + LANES:3 * w + 2 * LANES]), g2_ref[...])
    wlog = -_softplus(-(w0_ref[...] + lora[:, :2 * w])) - 0.5
    decay = jnp.exp(-jnp.exp(wlog))
    aa = jax.nn.sigmoid(a0_ref[...] + lora[:, 2 * w:])
    ones = ones_ref[...]
    kk = k * kk_ref[...]
    kk = kk * lax.rsqrt(_dot_f32(kk * kk, ones) + 1e-12)
    ka = ka_ref[...]
    kd_f = k * (1.0 + (aa[:, :w] - 1.0) * ka)
    kd_b = k * (1.0 + (aa[:, w:] - 1.0) * ka)
    a_out[...] = -kk
    r_out[...] = r
    v_out[...] = v
    zf_out[:, 0:w] = decay[:, :w]
    zf_out[:, w:2 * w] = kk * aa[:, :w]
    zf_out[:, 2 * w:] = kd_f
    zb_out[:, 0:w] = decay[:, w:]
    zb_out[:, w:2 * w] = kk * aa[:, w:]
    zb_out[:, 2 * w:] = kd_b
    gb_out[:, 0:w] = g
    gb_out[:, w:] = _dot_f32(r * (kd_f + kd_b) * rk_ref[...], ones) * v


def rwkv_prep(ub, blocks_per_batch, ctx_blocks, mu, w0, w2, a0, a2, g2, k_k, k_a, r_k):
    nt = ub.shape[0]
    w = RW_W
    lw = jnp.zeros((LANES, 4 * w), F32)
    lw = lw.at[0:32, 0:w].set(w2[0]).at[32:64, w:2 * w].set(w2[1])
    lw = lw.at[64:96, 2 * w:3 * w].set(a2[0]).at[96:128, 3 * w:].set(a2[1])
    g2p = jnp.pad(g2, ((0, LANES - RW_GATE_LORA), (0, 0)))
    mup = jnp.pad(mu, (0, RW_PAD - RW_COLS)).reshape(1, RW_PAD)
    head = np.arange(w) // HEAD_DIM
    ones = jnp.asarray((head[:, None] == head[None, :]).astype(np.float32))
    const = lambda shape: pl.BlockSpec(shape, lambda i: (0, 0))
    row = lambda n: pl.BlockSpec((TM, n), lambda i: (i, 0))
    return pl.pallas_call(
        functools.partial(_rwkv_prep_kernel, blocks_per_batch=blocks_per_batch, ctx_blocks=ctx_blocks),
        grid=(nt // TM,),
        in_specs=_halo_specs(nt, RW_PAD) + [const((1, RW_PAD)), const((LANES, 4 * w)), const((LANES, w)),
                                            const((1, 2 * w)), const((1, 2 * w)), const((1, w)), const((1, w)),
                                            const((1, w)), const((w, w))],
        out_specs=[row(w), row(w), row(w), row(3 * w), row(3 * w), row(2 * w)],
        out_shape=[jax.ShapeDtypeStruct((nt, n), F32) for n in (w, w, w, 3 * w, 3 * w, 2 * w)],
        compiler_params=_cparams("parallel"),
        name="rwkv_prep",
    )(ub, ub, ub, mup, lw, g2p, w0.reshape(1, 2 * w), a0.reshape(1, 2 * w), k_k.reshape(1, w),
      k_a.reshape(1, w), r_k.reshape(1, w), ones)


def _rwkv_scan_kernel(af_ref, afn_ref, ab_ref, abn_ref, rf_ref, rb_ref, vf_ref, vb_ref, zf_ref, zb_ref,
                      yf_ref, yb_ref, s_ref, sa_ref, yraw_ref, *, tblk):
    @pl.when(pl.program_id(0) == 0)
    def _init():
        s_ref[...] = jnp.zeros_like(s_ref)
        sa_ref[...] = jnp.zeros_like(sa_ref)
        yraw_ref[...] = jnp.zeros_like(yraw_ref)

    quarter = LANES // RW_KQ

    def over_keys(p):
        ax = p.ndim - 1
        return (p + pltpu.roll(p, quarter, ax) + pltpu.roll(p, 2 * quarter, ax)
                + pltpu.roll(p, 3 * quarter, ax))

    def rows(ref_row):
        return jnp.broadcast_to(ref_row, (HEAD_DIM, LANES))

    def one_direction(d, tc, tn, at_end, a_ref, an_ref, r_ref, v_ref, z_ref, sa):
        v_t = v_ref[tc]
        y = jnp.zeros((HEAD_DIM, LANES), F32)
        sa_next = jnp.zeros((HEAD_DIM, LANES), F32)
        for j in range(RW_KJ):
            js = slice(j, j + 1)
            s_new = (s_ref[d, j] * rows(z_ref[tc, 0, js, :]) + sa * rows(z_ref[tc, 1, js, :])
                     + v_t * rows(z_ref[tc, 2, js, :]))
            s_ref[d, j] = s_new
            y = y + s_new * rows(r_ref[tc, js, :])
            sa_next = sa_next + s_new * rows(jnp.where(at_end, an_ref[0, js, :], a_ref[tn, js, :]))
        return sa_next, y

    lane_y = lax.broadcasted_iota(jnp.int32, (RW_KJ, LANES), 1)
    lower_half = lane_y < 2 * quarter
    even_quarter = (lane_y & quarter) == 0

    def finish_rows(y):
        b0, b1, b2, b3 = (y[q * RW_KJ:(q + 1) * RW_KJ] for q in range(RW_KQ))
        s02 = jnp.where(lower_half, b0, b2) + pltpu.roll(jnp.where(lower_half, b2, b0), 2 * quarter, 1)
        s13 = jnp.where(lower_half, b1, b3) + pltpu.roll(jnp.where(lower_half, b3, b1), 2 * quarter, 1)
        return jnp.where(even_quarter, s02 + pltpu.roll(s02, 3 * quarter, 1), s13 + pltpu.roll(s13, quarter, 1))

    def step(t, carry):
        sa_f, part_b = carry
        tb = tblk - 1 - t
        at_end = t == tblk - 1
        sa_b = over_keys(part_b)
        tb_prev = jnp.minimum(tb + 1, tblk - 1)
        yb_ref[tb_prev] = finish_rows(yraw_ref[tb_prev])
        part_f, y_f = one_direction(0, t, jnp.minimum(t + 1, tblk - 1), at_end, af_ref, afn_ref, rf_ref,
                                    vf_ref, zf_ref, sa_f)
        yf_ref[t] = finish_rows(y_f)
        part_b, y_b = one_direction(1, tb, jnp.maximum(tb - 1, 0), at_end, ab_ref, abn_ref, rb_ref, vb_ref,
                                    zb_ref, sa_b)
        yraw_ref[tb] = y_b
        return over_keys(part_f), part_b

    sa_f, part_b = lax.fori_loop(0, tblk, step, (sa_ref[0], sa_ref[1]))
    sa_ref[0] = sa_f
    sa_ref[1] = part_b
    yb_ref[0] = finish_rows(yraw_ref[0])


def rwkv_scan(a_s, r_s, v_s, zf_s, zb_s, ctx_len, tblk=RW_TBLK):
    t = a_s.shape[0]
    nblk = t // tblk
    ncb = ctx_len // tblk
    kh = RW_KJ

    def rev(i):
        return jnp.where(i < ncb, ncb - 1 - i, nblk - 1 + ncb - i)

    def fwd3(i):
        return (i, 0, 0)

    def bwd3(i):
        return (rev(i), 0, 0)

    return pl.pallas_call(
        functools.partial(_rwkv_scan_kernel, tblk=tblk),
        grid=(nblk,),
        in_specs=[pl.BlockSpec((tblk, kh, LANES), fwd3),
                  pl.BlockSpec((1, kh, LANES), lambda i: (jnp.minimum(i + 1, nblk - 1) * tblk, 0, 0)),
                  pl.BlockSpec((tblk, kh, LANES), bwd3),
                  pl.BlockSpec((1, kh, LANES), lambda i: (rev(jnp.minimum(i + 1, nblk - 1)) * tblk + tblk - 1, 0, 0)),
                  pl.BlockSpec((tblk, kh, LANES), fwd3),
                  pl.BlockSpec((tblk, kh, LANES), bwd3),
                  pl.BlockSpec((tblk, HEAD_DIM, LANES), fwd3),
                  pl.BlockSpec((tblk, HEAD_DIM, LANES), bwd3),
                  pl.BlockSpec((tblk, 3, kh, LANES), lambda i: (i, 0, 0, 0)),
                  pl.BlockSpec((tblk, 3, kh, LANES), lambda i: (rev(i), 0, 0, 0))],
        out_specs=[pl.BlockSpec((tblk, kh, LANES), fwd3), pl.BlockSpec((tblk, kh, LANES), bwd3)],
        out_shape=[jax.ShapeDtypeStruct((t, kh, LANES), F32)] * 2,
        scratch_shapes=[pltpu.VMEM((2, kh, HEAD_DIM, LANES), F32), pltpu.VMEM((2, HEAD_DIM, LANES), F32),
                        pltpu.VMEM((tblk, HEAD_DIM, LANES), F32)],
        compiler_params=_cparams("arbitrary"),
        name="rwkv_scan",
    )(a_s, a_s, a_s, a_s, r_s, r_s, v_s, v_s, zf_s, zb_s)


def _rwkv_post_kernel(yf_ref, yb_ref, gb_ref, lw_ref, lb_ref, ones_ref, o_ref):
    w = RW_W
    ones = ones_ref[...]
    y = yf_ref[...] + yb_ref[...]
    yc = y - _dot_f32(y, ones) * (1.0 / HEAD_DIM)
    var = _dot_f32(yc * yc, ones) * (1.0 / HEAD_DIM)
    yn = yc * lax.rsqrt(var + RW_GN_EPS) * lw_ref[...] + lb_ref[...]
    o_ref[...] = ((yn + gb_ref[:, w:]) * gb_ref[:, 0:w]).astype(o_ref.dtype)


def rwkv_post(yf, yb, gb, ln_w, ln_b):
    nt, w = yf.shape
    head = np.arange(w) // HEAD_DIM
    ones = jnp.asarray((head[:, None] == head[None, :]).astype(np.float32))
    row = lambda n: pl.BlockSpec((TM, n), lambda i: (i, 0))
    const = lambda shape: pl.BlockSpec(shape, lambda i: (0, 0))
    return pl.pallas_call(
        _rwkv_post_kernel,
        grid=(nt // TM,),
        in_specs=[row(w), row(w), row(2 * w), const((1, w)), const((1, w)), const((w, w))],
        out_specs=row(w),
        out_shape=jax.ShapeDtypeStruct((nt, w), BF16),
        compiler_params=_cparams("parallel"),
        name="rwkv_post",
    )(yf, yb, gb, ln_w.reshape(1, w), ln_b.reshape(1, w), ones)


def _rw_keys_to_scan(z, bsz, nq):
    t = z.shape[0] // bsz
    z = z.reshape(bsz, t, nq, RW_HEADS, RW_KQ, RW_KJ)
    z = jnp.transpose(z, (1, 2, 5, 4, 0, 3))
    return z.reshape(t, nq, RW_KJ, LANES)


def _rw_vals_to_scan(v, bsz):
    t = v.shape[0] // bsz
    z = jnp.transpose(v.reshape(bsz, t, RW_HEADS, HEAD_DIM), (1, 3, 0, 2))
    z = jnp.broadcast_to(z[:, :, None], (t, HEAD_DIM, RW_KQ, bsz, RW_HEADS))
    return z.reshape(t, HEAD_DIM, LANES)


def _rw_vals_from_scan(y, bsz):
    t = y.shape[0]
    z = y.reshape(t, RW_KJ, RW_KQ, bsz, RW_HEADS)
    z = jnp.transpose(z, (3, 0, 4, 2, 1))
    return z.reshape(bsz * t, RW_W)


def rwkv_mixer(ub, bsz, ctx_len, mu, w0, w2, a0, a2, g2, k_k, k_a, r_k, ln_w, ln_b, tblk=RW_TBLK):
    tp = ub.shape[0] // bsz
    a_t, r_t, v_t, zf_t, zb_t, gb = rwkv_prep(ub, tp // TM, ctx_len // TM, mu, w0, w2, a0, a2, g2, k_k, k_a, r_k)
    yf, yb = rwkv_scan(_rw_keys_to_scan(a_t, bsz, 1)[:, 0], _rw_keys_to_scan(r_t, bsz, 1)[:, 0],
                       _rw_vals_to_scan(v_t, bsz), _rw_keys_to_scan(zf_t, bsz, 3), _rw_keys_to_scan(zb_t, bsz, 3),
                       ctx_len, tblk)
    return rwkv_post(_rw_vals_from_scan(yf, bsz), _rw_vals_from_scan(yb, bsz), gb, ln_w, ln_b)


def _dot_mask_f32(m, x):
    hi = x.astype(BF16)
    r1 = x - hi.astype(F32)
    mid = r1.astype(BF16)
    lo = (r1 - mid.astype(F32)).astype(BF16)
    return _dot(m, hi) + _dot(m, mid) + _dot(m, lo)


def _ssd_chunk(d, x_ref, bt_ref, c_ref, b_ref, dt_ref, dtt_ref, dth_ref, a_ref, at_ref, ah_ref, y_ref, st_ref):
    q = SSM_CHUNK
    row = lax.broadcasted_iota(jnp.int32, (q, q), 0)
    col = lax.broadcasted_iota(jnp.int32, (q, q), 1)
    mask = row >= col if d == 0 else row <= col
    tri = mask.astype(F32)
    acs = _dot_f32(tri, dt_ref[0, 0] * a_ref[d])
    acs_t = lax.dot_general(dtt_ref[0, 0] * at_ref[d], tri, (((1,), (1,)), ((), ())),
                            precision=lax.Precision.HIGHEST, preferred_element_type=F32)
    dt_h = dth_ref[0]
    dta_h = dt_h * ah_ref[d]
    acs_h = _dot_mask_f32(tri.astype(BF16), dta_h)
    tot_h = jnp.sum(dta_h, axis=0, keepdims=True)
    xdt = x_ref[0] * dt_h
    xdt_bf = xdt.astype(BF16)
    xw_bf = (xdt * jnp.exp(tot_h - acs_h)).astype(BF16)
    eacs_h = jnp.exp(acs_h)
    etot_h = jnp.exp(tot_h)
    gw = SSM_HPG * HEAD_DIM
    pair_lane = lax.broadcasted_iota(jnp.int32, (q, 2 * HEAD_DIM), 1)
    for g in range(SSM_GROUPS):
        gs = slice(g * SSM_STATE, (g + 1) * SSM_STATE)
        gl = slice(g * gw, (g + 1) * gw)
        cm = c_ref[0, :, gs]
        bm = b_ref[0, :, gs]
        bt = bt_ref[0, gs, :]
        cb = _dot_nt(cm, bm)
        state = st_ref[d, g]
        y_off = _dot(cm, state.astype(BF16)) * eacs_h[:, gl]
        for pr in range(SSM_HPG // 2):
            e0 = g * SSM_HPG + 2 * pr
            ps = slice(e0 * HEAD_DIM, (e0 + 2) * HEAD_DIM)
            both = []
            for e in (e0, e0 + 1):
                seg = jnp.exp(jnp.where(mask, acs[:, e:e + 1] - acs_t[e:e + 1, :], MASK_NEG))
                both.append(_dot((cb * seg).astype(BF16), xdt_bf[:, ps]))
            y_diag = jnp.where(pair_lane < HEAD_DIM, both[0], both[1])
            y_ref[0, :, ps] = y_diag + y_off[:, 2 * pr * HEAD_DIM:(2 * pr + 2) * HEAD_DIM]
        st_ref[d, g] = state * etot_h[:, gl] + _dot(bt, xw_bf[:, gl])


def _ssd_kernel(xf_ref, btf_ref, cf_ref, bf_ref, dtf_ref, dttf_ref, dthf_ref,
                xb_ref, btb_ref, cb_ref, bb_ref, dtb_ref, dttb_ref, dthb_ref,
                a_ref, at_ref, ah_ref, yf_ref, yb_ref, st_ref):
    @pl.when(pl.program_id(1) == 0)
    def _init():
        st_ref[...] = jnp.zeros_like(st_ref)

    _ssd_chunk(0, xf_ref, btf_ref, cf_ref, bf_ref, dtf_ref, dttf_ref, dthf_ref, a_ref, at_ref, ah_ref,
               yf_ref, st_ref)
    _ssd_chunk(1, xb_ref, btb_ref, cb_ref, bb_ref, dtb_ref, dttb_ref, dthb_ref, a_ref, at_ref, ah_ref,
               yb_ref, st_ref)


def ssd_scan(xs, bm, cm, dt, dth, a, ctx_len):
    bsz, t, _ = xs.shape
    q = SSM_CHUNK
    nc = t // q
    nctx = ctx_len // q
    bt = jnp.swapaxes(bm, 1, 2)
    dtt = jnp.swapaxes(dt, 2, 3)

    def cidx(d, j):
        return j if d == 0 else jnp.where(j < nctx, nctx - 1 - j, nc - 1 - (j - nctx))

    def specs(d):
        return [pl.BlockSpec((1, q, SSM_W), lambda b, j: (b, cidx(d, j), 0)),
                pl.BlockSpec((1, SSM_GN, q), lambda b, j: (b, 0, cidx(d, j))),
                pl.BlockSpec((1, q, SSM_GN), lambda b, j: (b, cidx(d, j), 0)),
                pl.BlockSpec((1, q, SSM_GN), lambda b, j: (b, cidx(d, j), 0)),
                pl.BlockSpec((1, 1, q, SSM_HEADS), lambda b, j: (b, d, cidx(d, j), 0)),
                pl.BlockSpec((1, 1, SSM_HEADS, q), lambda b, j: (b, d, 0, cidx(d, j))),
                pl.BlockSpec((1, q, SSM_W), lambda b, j: (b, cidx(d, j), d))]

    dir_args = (xs, bt, cm, bm, dt, dtt, dth)
    whole = lambda shape: pl.BlockSpec(shape, lambda b, j: (0, 0, 0))
    return pl.pallas_call(
        _ssd_kernel,
        grid=(bsz, nc),
        in_specs=specs(0) + specs(1) + [whole((2, 1, SSM_HEADS)), whole((2, SSM_HEADS, 1)), whole((2, 1, SSM_W))],
        out_specs=[pl.BlockSpec((1, q, SSM_W), lambda b, j: (b, cidx(0, j), 0)),
                   pl.BlockSpec((1, q, SSM_W), lambda b, j: (b, cidx(1, j), 0))],
        out_shape=[jax.ShapeDtypeStruct((bsz, t, SSM_W), F32)] * 2,
        scratch_shapes=[pltpu.VMEM((2, SSM_GROUPS, SSM_STATE, SSM_HPG * HEAD_DIM), F32)],
        compiler_params=_cparams("parallel", "arbitrary"),
        name="ssd_scan",
    )(*dir_args, *dir_args, a.reshape(2, 1, SSM_HEADS), a.reshape(2, SSM_HEADS, 1),
      jnp.repeat(a, HEAD_DIM, axis=1).reshape(2, 1, SSM_W))


def _ssm_prep_kernel(u_ref, up_ref, un_ref, cw_ref, cb_ref, dtb_ref, spread_ref, xs_out, b_out, c_out, dt_out,
                     dth_out, *, blocks_per_batch, ctx_blocks):
    first, last = _segment_edges(pl.program_id(0), blocks_per_batch, ctx_blocks)
    lo, hi = SSM_W, SSM_W + SSM_XBC
    xe = _with_halo(u_ref[:, lo:hi], up_ref[:, lo:hi], un_ref[:, lo:hi], first, last)
    acc = cb_ref[...] + cw_ref[0:1, :] * _shifted(xe, SSM_CONV // 2)
    for i in range(1, SSM_CONV):
        acc = acc + cw_ref[i:i + 1, :] * _shifted(xe, SSM_CONV // 2 - i)
    xbc = acc * jax.nn.sigmoid(acc)
    xs_out[...] = xbc[:, :SSM_W]
    b_out[...] = xbc[:, SSM_W:SSM_W + SSM_GN].astype(b_out.dtype)
    c_out[...] = xbc[:, SSM_W + SSM_GN:].astype(c_out.dtype)
    dt = _softplus(u_ref[:, hi:] + dtb_ref[...])
    dt_out[...] = dt
    dth_out[...] = _dot_f32(dt, spread_ref[...])


def ssm_prep(uc, blocks_per_batch, ctx_blocks, conv_w, conv_b, dt_bias):
    nt = uc.shape[0]
    npad = SSM_PAD - SSM_W - SSM_XBC
    dtb = jnp.pad(dt_bias.reshape(-1), (0, npad - 2 * SSM_HEADS)).reshape(1, npad)
    spread = np.zeros((npad, 2 * SSM_W), np.float32)
    spread[:2 * SSM_HEADS] = np.repeat(np.eye(2 * SSM_HEADS, dtype=np.float32), HEAD_DIM, axis=1)
    const = lambda shape: pl.BlockSpec(shape, lambda i: (0, 0))
    row = lambda n: pl.BlockSpec((TM, n), lambda i: (i, 0))
    return pl.pallas_call(
        functools.partial(_ssm_prep_kernel, blocks_per_batch=blocks_per_batch, ctx_blocks=ctx_blocks),
        grid=(nt // TM,),
        in_specs=_halo_specs(nt, SSM_PAD) + [const((SSM_CONV, SSM_XBC)), const((1, SSM_XBC)), const((1, npad)),
                                             const((npad, 2 * SSM_W))],
        out_specs=[row(SSM_W), row(SSM_GN), row(SSM_GN), row(npad), row(2 * SSM_W)],
        out_shape=[jax.ShapeDtypeStruct((nt, SSM_W), F32), jax.ShapeDtypeStruct((nt, SSM_GN), BF16),
                   jax.ShapeDtypeStruct((nt, SSM_GN), BF16), jax.ShapeDtypeStruct((nt, npad), F32),
                   jax.ShapeDtypeStruct((nt, 2 * SSM_W), F32)],
        compiler_params=_cparams("parallel"),
        name="ssm_prep",
    )(uc, uc, uc, conv_w, conv_b.reshape(1, SSM_XBC), dtb, jnp.asarray(spread))


def _ssm_post_kernel(yf_ref, yb_ref, xs_ref, z_ref, d_ref, nw_ref, o_ref):
    z = z_ref[...]
    y = (yf_ref[...] + yb_ref[...] + d_ref[...] * xs_ref[...]) * (z * jax.nn.sigmoid(z))
    gw = SSM_HPG * HEAD_DIM
    for g in range(SSM_GROUPS):
        gs = slice(g * gw, (g + 1) * gw)
        yg = y[:, gs]
        ms = jnp.mean(yg * yg, -1, keepdims=True)
        o_ref[:, gs] = (yg * lax.rsqrt(ms + RMS_EPS) * nw_ref[:, gs]).astype(o_ref.dtype)


def ssm_post(yf, yb, xs, uc, d_skip, norm_w):
    nt = xs.shape[0]
    row = pl.BlockSpec((TM, SSM_W), lambda i: (i, 0))
    const = pl.BlockSpec((1, SSM_W), lambda i: (0, 0))
    return pl.pallas_call(
        _ssm_post_kernel,
        grid=(nt // TM,),
        in_specs=[row, row, row, row, const, const],
        out_specs=row,
        out_shape=jax.ShapeDtypeStruct((nt, SSM_W), BF16),
        compiler_params=_cparams("parallel"),
        name="ssm_post",
    )(yf, yb, xs, uc, jnp.repeat(d_skip, HEAD_DIM).reshape(1, SSM_W), norm_w.reshape(1, SSM_W))


def ssm_mixer(uc, bsz, ctx_len, conv_w, conv_b, dt_bias, a_log, d_skip, norm_w):
    nt = uc.shape[0]
    tp = nt // bsz
    xs, bm, cm, dtp, dth = ssm_prep(uc, tp // TM, ctx_len // TM, conv_w, conv_b, dt_bias)
    dt = jnp.transpose(dtp[:, :2 * SSM_HEADS].reshape(bsz, tp, 2, SSM_HEADS), (0, 2, 1, 3))
    a = -jnp.exp(a_log.astype(F32)).reshape(2, SSM_HEADS)
    yf, yb = ssd_scan(xs.reshape(bsz, tp, SSM_W), bm.reshape(bsz, tp, SSM_GN), cm.reshape(bsz, tp, SSM_GN),
                      dt, dth.reshape(bsz, tp, 2 * SSM_W), a, ctx_len)
    return ssm_post(yf.reshape(nt, SSM_W), yb.reshape(nt, SSM_W), xs, uc, d_skip, norm_w)


def _adaln_router_kernel(x_ref, sh_ref, sc_ref, wr_ref, f_ref, lg_ref):
    xm = x_ref[...] * (1.0 + sc_ref[0]) + sh_ref[0]
    f_ref[...] = xm.astype(f_ref.dtype)
    lg_ref[...] = jnp.dot(xm, wr_ref[...], precision=lax.Precision.HIGHEST, preferred_element_type=F32)


def adaln_router(x, mod_tab, sh_idx, sc_idx, wr):
    nt, d = x.shape
    n = wr.shape[1]
    return pl.pallas_call(
        _adaln_router_kernel,
        grid=(nt // TM,),
        in_specs=[pl.BlockSpec((TM, d), lambda i: (i, 0))] + _mod_specs(d, sh_idx, sc_idx)
        + [pl.BlockSpec((d, n), lambda i: (0, 0))],
        out_specs=[pl.BlockSpec((TM, d), lambda i: (i, 0)), pl.BlockSpec((TM, n), lambda i: (i, 0))],
        out_shape=[jax.ShapeDtypeStruct((nt, d), BF16), jax.ShapeDtypeStruct((nt, n), F32)],
        compiler_params=_cparams("parallel"),
        name="adaln_router",
    )(x, mod_tab, mod_tab, wr)


def _expert_changed(be_ref, i):
    return jnp.logical_or(i == 0, be_ref[i] != be_ref[jnp.maximum(i - 1, 0)])


def _moe_up_kernel(be_ref, x_ref, wg_ref, wu_ref, o_ref, wg_bf, wu_bf):
    @pl.when(_expert_changed(be_ref, pl.program_id(1)))
    def _cast():
        wg_bf[...] = wg_ref[0, 0].astype(BF16)
        wu_bf[...] = wu_ref[0, 0].astype(BF16)

    x = x_ref[...]
    g = _dot(x, wg_bf[...])
    u = _dot(x, wu_bf[...])
    o_ref[...] = (g * jax.nn.sigmoid(g) * u).astype(o_ref.dtype)


def moe_up(xb, block_e, wg, wu, layer, n_split=2):
    ns, d = xb.shape
    f = wg.shape[3]
    tn = f // n_split
    return pl.pallas_call(
        _moe_up_kernel,
        grid_spec=pltpu.PrefetchScalarGridSpec(
            num_scalar_prefetch=1,
            grid=(n_split, ns // MOE_BLOCK),
            in_specs=[pl.BlockSpec((MOE_BLOCK, d), lambda n, i, be: (i, 0)),
                      pl.BlockSpec((1, 1, d, tn), lambda n, i, be: (layer, be[i], 0, n)),
                      pl.BlockSpec((1, 1, d, tn), lambda n, i, be: (layer, be[i], 0, n))],
            out_specs=pl.BlockSpec((MOE_BLOCK, tn), lambda n, i, be: (i, n)),
            scratch_shapes=[pltpu.VMEM((d, tn), BF16), pltpu.VMEM((d, tn), BF16)]),
        out_shape=jax.ShapeDtypeStruct((ns, f), BF16),
        compiler_params=_cparams("arbitrary", "arbitrary"),
        name="moe_up",
    )(block_e, xb, wg, wu)


def _moe_down_kernel(be_ref, a_ref, wd_ref, o_ref, wd_bf):
    @pl.when(_expert_changed(be_ref, pl.program_id(0)))
    def _cast():
        wd_bf[...] = wd_ref[0, 0].astype(BF16)

    o_ref[...] = _dot(a_ref[...], wd_bf[...])


def moe_down(act, block_e, wd, layer):
    ns, f = act.shape
    d = wd.shape[3]
    return pl.pallas_call(
        _moe_down_kernel,
        grid_spec=pltpu.PrefetchScalarGridSpec(
            num_scalar_prefetch=1,
            grid=(ns // MOE_BLOCK,),
            in_specs=[pl.BlockSpec((MOE_BLOCK, f), lambda i, be: (i, 0)),
                      pl.BlockSpec((1, 1, f, d), lambda i, be: (layer, be[i], 0, 0))],
            out_specs=pl.BlockSpec((MOE_BLOCK, d), lambda i, be: (i, 0)),
            scratch_shapes=[pltpu.VMEM((f, d), BF16)]),
        out_shape=jax.ShapeDtypeStruct((ns, d), F32),
        compiler_params=_cparams("arbitrary"),
        name="moe_down",
    )(block_e, act, wd)


def _moe_res_ln_kernel(h_ref, y0_ref, y1_ref, gt_ref, gate_ref, g_ref, b_ref, out_ref):
    gt = gt_ref[...]
    y = y0_ref[...] * gt[:, 0:1] + y1_ref[...] * gt[:, 1:2]
    z = DEEPNORM_ALPHA * h_ref[...] + gate_ref[0] * y
    out_ref[...] = _layer_norm(z, g_ref[...], b_ref[...])


def moe_res_ln(h, y0, y1, gates, mod_tab, gate_idx, ln_g, ln_b):
    nt, d = h.shape
    row = pl.BlockSpec((TM, d), lambda i: (i, 0))
    return pl.pallas_call(
        _moe_res_ln_kernel,
        grid=(nt // TM,),
        in_specs=[row, row, row,
                  pl.BlockSpec((TM, TOP_K), lambda i: (i, 0)),
                  pl.BlockSpec((1, 1, d), lambda i: (i, 0, gate_idx)),
                  pl.BlockSpec((1, d), lambda i: (0, 0)),
                  pl.BlockSpec((1, d), lambda i: (0, 0))],
        out_specs=row,
        out_shape=jax.ShapeDtypeStruct((nt, d), F32),
        compiler_params=_cparams("parallel"),
        name="moe_res_ln",
    )(h, y0, y1, gates, mod_tab, ln_g.reshape(1, d), ln_b.reshape(1, d))


def moe_mixer(h, mod_tab, router_w, router_b, wg, wu, wd, layer, ln_g, ln_b):
    n_tok, d = h.shape
    wr = jnp.pad(router_w, ((0, 0), (0, ROUTER_PAD - N_EXPERTS)))
    f_bf, logits = adaln_router(h, mod_tab, 3, 4, wr)
    logits = logits[:, :N_EXPERTS] + router_b
    top_val, top_idx = lax.top_k(logits, TOP_K)
    gates = jax.nn.softmax(top_val, axis=-1)
    n_assign = n_tok * TOP_K
    flat_e = top_idx.reshape(-1)
    onehot = (flat_e[:, None] == jnp.arange(N_EXPERTS)[None, :]).astype(jnp.int32)
    csum = jnp.cumsum(onehot, axis=0)
    counts = csum[-1]
    rank = jnp.sum((csum - onehot) * onehot, axis=1)
    padded = (counts + MOE_BLOCK - 1) // MOE_BLOCK * MOE_BLOCK
    pad_end = jnp.cumsum(padded)
    pad_start = pad_end - padded
    dest = pad_start[flat_e] + rank
    n_blocks = -(-n_assign // MOE_BLOCK) + N_EXPERTS
    flat_tok = jnp.repeat(jnp.arange(n_tok, dtype=jnp.int32), TOP_K)
    slot_tok = jnp.full((n_blocks * MOE_BLOCK,), n_tok, jnp.int32).at[dest].set(flat_tok)
    block_e = jnp.minimum(jnp.searchsorted(pad_end, jnp.arange(n_blocks) * MOE_BLOCK, side='right'),
                          N_EXPERTS - 1).astype(jnp.int32)
    f_pad = jnp.concatenate([f_bf, jnp.zeros((1, d), f_bf.dtype)], axis=0)
    xb = f_pad[slot_tok]
    act = moe_up(xb, block_e, wg, wu, layer)
    yb = moe_down(act, block_e, wd, layer)
    dest2 = dest.reshape(n_tok, TOP_K)
    return moe_res_ln(h, yb[dest2[:, 0]], yb[dest2[:, 1]], gates, mod_tab, 5, ln_g, ln_b)


def kernel(x, c, ctx, c_ctx, w_mod, b_mod, w_in, w_out, ln1_g, ln1_b, ln2_g, ln2_b, na_rpb, rw_mu, rw_w0, rw_w2, rw_a0, rw_a2, rw_g2, rw_k_k, rw_k_a, rw_r_k, rw_ln_w, rw_ln_b, ssm_conv_w, ssm_conv_b, ssm_dt_bias, ssm_a_log, ssm_d, ssm_norm_w, ffn_w_gate, ffn_w_up, ffn_w_down, moe_router, moe_router_b, moe_w_gate, moe_w_up, moe_w_down):
    bsz, seq, d = x.shape
    ctx_len = ctx.shape[1]
    tp = ctx_len + seq
    nt = bsz * tp
    assert ctx_len % TM == 0 and seq % TM == 0 and RW_KQ * bsz * RW_HEADS == LANES
    rows = seq // GRID_W
    depth = w_mod.shape[0]

    h = jnp.concatenate([ctx, x], axis=1).reshape(nt, d)

    n_mod_rows = -(-(bsz + 1) // 16) * 16
    cin = jnp.concatenate([c, c_ctx[None, :], jnp.zeros((n_mod_rows - bsz - 1, d), F32)], axis=0)
    mod_all = modulation(cin, w_mod, b_mod)
    blk = np.arange(nt // TM)
    blk_row = np.where(blk % (tp // TM) < ctx_len // TM, bsz, blk // (tp // TM))

    for l in range(depth):
        mod_tab = mod_all[l][blk_row][:, None, :]
        wl = w_in[l]
        w_in_p = jnp.concatenate(
            [wl[:, :NA_COLS],
             jnp.pad(wl[:, NA_COLS:NA_COLS + RW_COLS], ((0, 0), (0, RW_PAD - RW_COLS))),
             jnp.pad(wl[:, NA_COLS + RW_COLS:], ((0, 0), (0, SSM_PAD - SSM_COLS)))], axis=1).astype(BF16)
        ua, ub, uc = adaln_proj(h, mod_tab, 0, 1, w_in_p, (NA_COLS, RW_PAD, SSM_PAD), (BF16, F32, F32))
        oa = na_attention(ua.reshape(bsz, tp, NA_COLS), _na_bias_table(na_rpb[l], rows), ctx_len)
        ob = rwkv_mixer(ub, bsz, ctx_len, rw_mu[l], rw_w0[l], rw_w2[l], rw_a0[l], rw_a2[l], rw_g2[l],
                        rw_k_k[l], rw_k_a[l], rw_r_k[l], rw_ln_w[l], rw_ln_b[l])
        oc = ssm_mixer(uc, bsz, ctx_len, ssm_conv_w[l], ssm_conv_b[l], ssm_dt_bias[l], ssm_a_log[l],
                       ssm_d[l], ssm_norm_w[l])
        h = mix_proj_res_ln(oa.reshape(nt, NA_W), ob, oc, w_out[l].astype(BF16), h, mod_tab, 2,
                            ln1_g[l], ln1_b[l])

        j = l // 2
        if l % 2 == 0:
            act = adaln_swiglu(h, mod_tab, 3, 4, ffn_w_gate[j].astype(BF16), ffn_w_up[j].astype(BF16))
            h = proj_res_ln(act, ffn_w_down[j].astype(BF16), h, mod_tab, 5, ln2_g[l], ln2_b[l])
        else:
            h = moe_mixer(h, mod_tab, moe_router[j], moe_router_b[j], moe_w_gate, moe_w_up, moe_w_down, j,
                          ln2_g[l], ln2_b[l])
    return h.reshape(bsz, tp, d)[:, ctx_len:]
```

```python
import functools

import numpy as np
import jax
import jax.numpy as jnp
from jax import lax
from jax.experimental import pallas as pl
from jax.experimental.pallas import tpu as pltpu

F32 = jnp.float32
BF16 = jnp.bfloat16

D_MODEL = 1024
DEPTH = 4
GRID_W = 64
HEAD_DIM = 64
NA_HEADS = 4
RW_HEADS = 4
SSM_HEADS = 8
NA_W = NA_HEADS * HEAD_DIM
RW_W = RW_HEADS * HEAD_DIM
SSM_W = SSM_HEADS * HEAD_DIM
NA_WIN_R = 8
NA_WIN_C = 16
RW_DECAY_LORA = 32
RW_AAA_LORA = 32
RW_GATE_LORA = 64
RW_GN_EPS = 64e-5
SSM_GROUPS = 2
SSM_HPG = SSM_HEADS // SSM_GROUPS
SSM_STATE = 128
SSM_CONV = 5
SSM_CHUNK = 128
SSM_GN = SSM_GROUPS * SSM_STATE
SSM_XBC = SSM_W + 2 * SSM_GN
NA_COLS = 3 * NA_W
RW_COLS = 3 * RW_W + 2 * RW_DECAY_LORA + 2 * RW_AAA_LORA + RW_GATE_LORA
SSM_COLS = SSM_W + SSM_XBC + 2 * SSM_HEADS
N_EXPERTS = 8
TOP_K = 2
MOE_BLOCK = 256
DEEPNORM_ALPHA = (2 * DEPTH) ** 0.25
LN_EPS = 1e-5
RMS_EPS = 1e-5

LANES = 128
TM = 256
RW_PAD = 1024
SSM_PAD = 1664
ROUTER_PAD = LANES
VMEM_LIMIT = 56 * 1024 * 1024
MASK_NEG = -1e30


def _cparams(*sem):
    return pltpu.CompilerParams(dimension_semantics=sem, vmem_limit_bytes=VMEM_LIMIT)


def _dot(a, b):
    return jnp.dot(a, b, preferred_element_type=F32)


def _dot_nt(a, b):
    return lax.dot_general(a, b, (((1,), (1,)), ((), ())), preferred_element_type=F32)


def _layer_norm(z, g, b):
    mu = jnp.mean(z, -1, keepdims=True)
    zc = z - mu
    var = jnp.mean(zc * zc, -1, keepdims=True)
    return zc * lax.rsqrt(var + LN_EPS) * g + b


def _mod_kernel(c_ref, w_ref, b_ref, o_ref):
    cs = c_ref[...]
    cs = cs * jax.nn.sigmoid(cs)
    o_ref[0] = _dot(cs.astype(BF16), w_ref[0].astype(BF16)) + b_ref[0]


def modulation(cin, w_mod, b_mod):
    r, d = cin.shape
    nl, _, n = w_mod.shape
    tn = d
    return pl.pallas_call(
        _mod_kernel,
        grid=(nl, n // tn),
        in_specs=[pl.BlockSpec((r, d), lambda l, j: (0, 0)),
                  pl.BlockSpec((1, d, tn), lambda l, j: (l, 0, j)),
                  pl.BlockSpec((1, 1, tn), lambda l, j: (l, 0, j))],
        out_specs=pl.BlockSpec((1, r, tn), lambda l, j: (l, 0, j)),
        out_shape=jax.ShapeDtypeStruct((nl, r, n), F32),
        compiler_params=_cparams("parallel", "parallel"),
        name="modulation",
    )(cin, w_mod, b_mod.reshape(nl, 1, n))


def _mod_specs(d, idx_a, idx_b):
    return [pl.BlockSpec((1, 1, d), lambda i: (i, 0, idx_a)),
            pl.BlockSpec((1, 1, d), lambda i: (i, 0, idx_b))]


def _adaln_proj_kernel(x_ref, sh_ref, sc_ref, w_ref, *o_refs, splits):
    xm = x_ref[...] * (1.0 + sc_ref[0]) + sh_ref[0]
    y = _dot(xm.astype(BF16), w_ref[...])
    off = 0
    for o_ref, n in zip(o_refs, splits):
        o_ref[...] = y[:, off:off + n].astype(o_ref.dtype)
        off += n


def adaln_proj(x, mod_tab, sh_idx, sc_idx, w, splits, dtypes):
    nt, d = x.shape
    n = w.shape[1]
    return pl.pallas_call(
        functools.partial(_adaln_proj_kernel, splits=splits),
        grid=(nt // TM,),
        in_specs=[pl.BlockSpec((TM, d), lambda i: (i, 0))] + _mod_specs(d, sh_idx, sc_idx)
        + [pl.BlockSpec((d, n), lambda i: (0, 0))],
        out_specs=[pl.BlockSpec((TM, s), lambda i: (i, 0)) for s in splits],
        out_shape=[jax.ShapeDtypeStruct((nt, s), dt) for s, dt in zip(splits, dtypes)],
        compiler_params=_cparams("parallel"),
        name="adaln_proj",
    )(x, mod_tab, mod_tab, w)


def _adaln_swiglu_kernel(x_ref, sh_ref, sc_ref, wg_ref, wu_ref, o_ref):
    xm = (x_ref[...] * (1.0 + sc_ref[0]) + sh_ref[0]).astype(BF16)
    g = _dot(xm, wg_ref[...])
    u = _dot(xm, wu_ref[...])
    o_ref[...] = (g * jax.nn.sigmoid(g) * u).astype(o_ref.dtype)


def adaln_swiglu(x, mod_tab, sh_idx, sc_idx, wg, wu):
    nt, d = x.shape
    f = wg.shape[1]
    return pl.pallas_call(
        _adaln_swiglu_kernel,
        grid=(nt // TM,),
        in_specs=[pl.BlockSpec((TM, d), lambda i: (i, 0))] + _mod_specs(d, sh_idx, sc_idx)
        + [pl.BlockSpec((d, f), lambda i: (0, 0)), pl.BlockSpec((d, f), lambda i: (0, 0))],
        out_specs=pl.BlockSpec((TM, f), lambda i: (i, 0)),
        out_shape=jax.ShapeDtypeStruct((nt, f), BF16),
        compiler_params=_cparams("parallel"),
        name="adaln_swiglu",
    )(x, mod_tab, mod_tab, wg, wu)


def _proj_res_ln_kernel(o_ref, w_ref, h_ref, gate_ref, g_ref, b_ref, out_ref):
    y = _dot(o_ref[...], w_ref[...])
    z = DEEPNORM_ALPHA * h_ref[...] + gate_ref[0] * y
    out_ref[...] = _layer_norm(z, g_ref[...], b_ref[...])


def proj_res_ln(o, w, h, mod_tab, gate_idx, ln_g, ln_b):
    nt, k = o.shape
    d = w.shape[1]
    return pl.pallas_call(
        _proj_res_ln_kernel,
        grid=(nt // TM,),
        in_specs=[pl.BlockSpec((TM, k), lambda i: (i, 0)),
                  pl.BlockSpec((k, d), lambda i: (0, 0)),
                  pl.BlockSpec((TM, d), lambda i: (i, 0)),
                  pl.BlockSpec((1, 1, d), lambda i: (i, 0, gate_idx)),
                  pl.BlockSpec((1, d), lambda i: (0, 0)),
                  pl.BlockSpec((1, d), lambda i: (0, 0))],
        out_specs=pl.BlockSpec((TM, d), lambda i: (i, 0)),
        out_shape=jax.ShapeDtypeStruct((nt, d), F32),
        compiler_params=_cparams("parallel"),
        name="proj_res_ln",
    )(o, w, h, mod_tab, ln_g.reshape(1, d), ln_b.reshape(1, d))


def _mix_proj_res_ln_kernel(oa_ref, ob_ref, oc_ref, w_ref, h_ref, gate_ref, g_ref, b_ref, out_ref):
    s1, s2 = NA_W, NA_W + RW_W
    y = (_dot(oa_ref[...], w_ref[0:s1, :]) + _dot(ob_ref[...], w_ref[s1:s2, :])
         + _dot(oc_ref[...], w_ref[s2:, :]))
    z = DEEPNORM_ALPHA * h_ref[...] + gate_ref[0] * y
    out_ref[...] = _layer_norm(z, g_ref[...], b_ref[...])


def mix_proj_res_ln(oa, ob, oc, w, h, mod_tab, gate_idx, ln_g, ln_b):
    nt, d = h.shape
    row = lambda n: pl.BlockSpec((TM, n), lambda i: (i, 0))
    return pl.pallas_call(
        _mix_proj_res_ln_kernel,
        grid=(nt // TM,),
        in_specs=[row(NA_W), row(RW_W), row(SSM_W),
                  pl.BlockSpec((d, d), lambda i: (0, 0)),
                  row(d),
                  pl.BlockSpec((1, 1, d), lambda i: (i, 0, gate_idx)),
                  pl.BlockSpec((1, d), lambda i: (0, 0)),
                  pl.BlockSpec((1, d), lambda i: (0, 0))],
        out_specs=row(d),
        out_shape=jax.ShapeDtypeStruct((nt, d), F32),
        compiler_params=_cparams("parallel"),
        name="mix_proj_res_ln",
    )(oa, ob, oc, w, h, mod_tab, ln_g.reshape(1, d), ln_b.reshape(1, d))


def _na_bias_table(rpb, rows):
    win_r = min(NA_WIN_R, rows)
    q = np.arange(GRID_W)
    col_start = np.clip(q - NA_WIN_C // 2, 0, GRID_W - NA_WIN_C)
    kc = np.arange(GRID_W)
    valid = (kc[None, :] >= col_start[:, None]) & (kc[None, :] < col_start[:, None] + NA_WIN_C)
    crel = np.clip(kc[None, :] - q[:, None] + NA_WIN_C - 1, 0, 2 * NA_WIN_C - 2)
    case = np.arange(NA_WIN_R)
    rrel = np.clip(NA_WIN_R - 1 - case[:, None] + np.arange(win_r)[None, :], 0, 2 * NA_WIN_R - 2)
    t = rpb[:, rrel]
    t = t[:, :, :, crel]
    t = jnp.transpose(t, (1, 0, 3, 2, 4))
    t = jnp.where(valid[None, None, :, None, :], t, MASK_NEG)
    return t.reshape(NA_WIN_R, NA_HEADS * GRID_W, win_r * GRID_W).astype(F32)


NA_ROWS_PER_STEP = 4


def _attend(q, keysets, scale):
    head = lax.broadcasted_iota(jnp.int32, (GRID_W, NA_W), 1) // HEAD_DIM
    zero = jnp.zeros_like(q)
    qs = jnp.concatenate([jnp.where(head == h, q, zero) for h in range(NA_HEADS)], axis=0)
    scores = []
    for keys, _, bias in keysets:
        s = _dot_nt(qs, keys) * scale
        scores.append(s if bias is None else s + bias)
    m = scores[0].max(-1, keepdims=True)
    for s in scores[1:]:
        m = jnp.maximum(m, s.max(-1, keepdims=True))
    den = 0.0
    acc = 0.0
    for s, (_, vals, _) in zip(scores, keysets):
        p = jnp.exp(s - m)
        den = den + p.sum(-1, keepdims=True)
        acc = acc + _dot(p.astype(BF16), vals)
    acc = acc / den
    out = jnp.where(head == 0, acc[0:GRID_W], 0.0)
    for h in range(1, NA_HEADS):
        out = out + jnp.where(head == h, acc[h * GRID_W:(h + 1) * GRID_W], 0.0)
    return out


def _na_kernel(u_ref, bias_ref, o_ref, *, ctx_len, rows, scale):
    j = pl.program_id(1)
    rps = NA_ROWS_PER_STEP
    n_ctx_steps = ctx_len // (rps * GRID_W)
    win_r = min(NA_WIN_R, rows)
    kc = u_ref[0, 0:ctx_len, NA_W:2 * NA_W]
    vc = u_ref[0, 0:ctx_len, 2 * NA_W:3 * NA_W]

    @pl.when(j < n_ctx_steps)
    def _ctx():
        for i in range(rps):
            q0 = pl.multiple_of((j * rps + i) * GRID_W, GRID_W)
            q = u_ref[0, pl.ds(q0, GRID_W), 0:NA_W]
            o_ref[0, i * GRID_W:(i + 1) * GRID_W, :] = _attend(q, [(kc, vc, None)], scale).astype(o_ref.dtype)

    @pl.when(j >= n_ctx_steps)
    def _lat():
        for i in range(rps):
            r = (j - n_ctx_steps) * rps + i
            r0 = jnp.clip(r - win_r // 2, 0, rows - win_r)
            q = u_ref[0, pl.ds(pl.multiple_of(ctx_len + r * GRID_W, GRID_W), GRID_W), 0:NA_W]
            kstart = pl.multiple_of(ctx_len + r0 * GRID_W, GRID_W)
            kl = u_ref[0, pl.ds(kstart, win_r * GRID_W), NA_W:2 * NA_W]
            vl = u_ref[0, pl.ds(kstart, win_r * GRID_W), 2 * NA_W:3 * NA_W]
            out = _attend(q, [(kl, vl, bias_ref[r - r0]), (kc, vc, None)], scale)
            o_ref[0, i * GRID_W:(i + 1) * GRID_W, :] = out.astype(o_ref.dtype)


def na_attention(ua, bias_tab, ctx_len):
    bsz, tp, _ = ua.shape
    rows = (tp - ctx_len) // GRID_W
    qn = NA_ROWS_PER_STEP * GRID_W
    assert ctx_len % qn == 0 and rows % NA_ROWS_PER_STEP == 0
    return pl.pallas_call(
        functools.partial(_na_kernel, ctx_len=ctx_len, rows=rows, scale=HEAD_DIM ** -0.5),
        grid=(bsz, tp // qn),
        in_specs=[pl.BlockSpec((1, tp, NA_COLS), lambda b, j: (b, 0, 0)),
                  pl.BlockSpec(bias_tab.shape, lambda b, j: (0, 0, 0))],
        out_specs=pl.BlockSpec((1, qn, NA_W), lambda b, j: (b, j, 0)),
        out_shape=jax.ShapeDtypeStruct((bsz, tp, NA_W), BF16),
        compiler_params=_cparams("parallel", "arbitrary"),
        name="na_attention",
    )(ua, bias_tab)


RW_KQ = 4
RW_KJ = HEAD_DIM // RW_KQ
RW_TBLK = 64
HALO = 8


def _segment_edges(i, blocks_per_batch, ctx_blocks):
    j = i % blocks_per_batch
    first = jnp.logical_or(j == 0, j == ctx_blocks)
    last = jnp.logical_or(j == ctx_blocks - 1, j == blocks_per_batch - 1)
    return first, last


def _with_halo(x, prev, nxt, first, last):
    return jnp.concatenate([jnp.where(first, 0.0, prev), x, jnp.where(last, 0.0, nxt)], axis=0)


def _shifted(xe, s):
    return pltpu.roll(xe, s % xe.shape[0], 0)[HALO:HALO + TM]


def _halo_specs(nt, width):
    return [pl.BlockSpec((TM, width), lambda i: (i, 0)),
            pl.BlockSpec((HALO, width), lambda i: (jnp.maximum(i * (TM // HALO) - 1, 0), 0)),
            pl.BlockSpec((HALO, width), lambda i: (jnp.minimum((i + 1) * (TM // HALO), nt // HALO - 1), 0))]


def _softplus(x):
    return jnp.maximum(x, 0.0) + jnp.log1p(jnp.exp(-jnp.abs(x)))


def _dot_f32(a, b):
    return jnp.dot(a, b, precision=lax.Precision.HIGHEST, preferred_element_type=F32)


def _rwkv_prep_kernel(u_ref, up_ref, un_ref, mu_ref, lw_ref, g2_ref, w0_ref, a0_ref, kk_ref, ka_ref, rk_ref,
                      ones_ref, a_out, r_out, v_out, zf_out, zb_out, gb_out, *, blocks_per_batch, ctx_blocks):
    first, last = _segment_edges(pl.program_id(0), blocks_per_batch, ctx_blocks)
    xe = _with_halo(u_ref[...], up_ref[...], un_ref[...], first, last)
    x = u_ref[...]
    um = x + mu_ref[...] * (0.5 * (_shifted(xe, 1) + _shifted(xe, -1)) - x)
    w = RW_W
    r, k, v = um[:, 0:w], um[:, w:2 * w], um[:, 2 * w:3 * w]
    lo = um[:, 3 * w:3 * w + LANES]
    lane = lax.broadcasted_iota(jnp.int32, lo.shape, 1)
    lo = jnp.where(lane < 2 * RW_DECAY_LORA, jnp.tanh(lo), lo)
    lora = _dot_f32(lo, lw_ref[...])
    g = _dot_f32(jax.nn.sigmoid(um[:, 3 * w + LANES:3 * w + 2 * LANES]), g2_ref[...])
    wlog = -_softplus(-(w0_ref[...] + lora[:, :2 * w])) - 0.5
    decay = jnp.exp(-jnp.exp(wlog))
    aa = jax.nn.sigmoid(a0_ref[...] + lora[:, 2 * w:])
    ones = ones_ref[...]
    kk = k * kk_ref[...]
    kk = kk * lax.rsqrt(_dot_f32_mask(kk * kk, ones) + 1e-12)
    ka = ka_ref[...]
    kd_f = k * (1.0 + (aa[:, :w] - 1.0) * ka)
    kd_b = k * (1.0 + (aa[:, w:] - 1.0) * ka)
    a_out[...] = -kk
    r_out[...] = r
    v_out[...] = v
    zf_out[:, 0:w] = decay[:, :w]
    zf_out[:, w:2 * w] = kk * aa[:, :w]
    zf_out[:, 2 * w:] = kd_f
    zb_out[:, 0:w] = decay[:, w:]
    zb_out[:, w:2 * w] = kk * aa[:, w:]
    zb_out[:, 2 * w:] = kd_b
    gb_out[:, 0:w] = g
    gb_out[:, w:] = _dot_f32_mask(r * (kd_f + kd_b) * rk_ref[...], ones) * v


def rwkv_prep(ub, blocks_per_batch, ctx_blocks, mu, w0, w2, a0, a2, g2, k_k, k_a, r_k):
    nt = ub.shape[0]
    w = RW_W
    lw = jnp.zeros((LANES, 4 * w), F32)
    lw = lw.at[0:32, 0:w].set(w2[0]).at[32:64, w:2 * w].set(w2[1])
    lw = lw.at[64:96, 2 * w:3 * w].set(a2[0]).at[96:128, 3 * w:].set(a2[1])
    g2p = jnp.pad(g2, ((0, LANES - RW_GATE_LORA), (0, 0)))
    mup = jnp.pad(mu, (0, RW_PAD - RW_COLS)).reshape(1, RW_PAD)
    head = np.arange(w) // HEAD_DIM
    ones = jnp.asarray(head[:, None] == head[None, :], BF16)
    const = lambda shape: pl.BlockSpec(shape, lambda i: (0, 0))
    row = lambda n: pl.BlockSpec((TM, n), lambda i: (i, 0))
    return pl.pallas_call(
        functools.partial(_rwkv_prep_kernel, blocks_per_batch=blocks_per_batch, ctx_blocks=ctx_blocks),
        grid=(nt // TM,),
        in_specs=_halo_specs(nt, RW_PAD) + [const((1, RW_PAD)), const((LANES, 4 * w)), const((LANES, w)),
                                            const((1, 2 * w)), const((1, 2 * w)), const((1, w)), const((1, w)),
                                            const((1, w)), const((w, w))],
        out_specs=[row(w), row(w), row(w), row(3 * w), row(3 * w), row(2 * w)],
        out_shape=[jax.ShapeDtypeStruct((nt, n), F32) for n in (w, w, w, 3 * w, 3 * w, 2 * w)],
        compiler_params=_cparams("parallel"),
        name="rwkv_prep",
    )(ub, ub, ub, mup, lw, g2p, w0.reshape(1, 2 * w), a0.reshape(1, 2 * w), k_k.reshape(1, w),
      k_a.reshape(1, w), r_k.reshape(1, w), ones)


def _rwkv_scan_kernel(af_ref, afn_ref, ab_ref, abn_ref, rf_ref, rb_ref, vf_ref, vb_ref, zf_ref, zb_ref,
                      yf_ref, yb_ref, s_ref, sa_ref, yraw_ref, *, tblk):
    @pl.when(pl.program_id(0) == 0)
    def _init():
        s_ref[...] = jnp.zeros_like(s_ref)
        sa_ref[...] = jnp.zeros_like(sa_ref)
        yraw_ref[...] = jnp.zeros_like(yraw_ref)

    quarter = LANES // RW_KQ

    def over_keys(p):
        ax = p.ndim - 1
        return (p + pltpu.roll(p, quarter, ax) + pltpu.roll(p, 2 * quarter, ax)
                + pltpu.roll(p, 3 * quarter, ax))

    def rows(ref_row):
        return jnp.broadcast_to(ref_row, (HEAD_DIM, LANES))

    def one_direction(d, tc, tn, at_end, a_ref, an_ref, r_ref, v_ref, z_ref, sa):
        v_t = v_ref[tc]
        y = jnp.zeros((HEAD_DIM, LANES), F32)
        sa_next = jnp.zeros((HEAD_DIM, LANES), F32)
        for j in range(RW_KJ):
            js = slice(j, j + 1)
            s_new = (s_ref[d, j] * rows(z_ref[tc, 0, js, :]) + sa * rows(z_ref[tc, 1, js, :])
                     + v_t * rows(z_ref[tc, 2, js, :]))
            s_ref[d, j] = s_new
            y = y + s_new * rows(r_ref[tc, js, :])
            sa_next = sa_next + s_new * rows(jnp.where(at_end, an_ref[0, js, :], a_ref[tn, js, :]))
        return sa_next, y

    lane_y = lax.broadcasted_iota(jnp.int32, (RW_KJ, LANES), 1)
    lower_half = lane_y < 2 * quarter
    even_quarter = (lane_y & quarter) == 0

    def finish_rows(y):
        b0, b1, b2, b3 = (y[q * RW_KJ:(q + 1) * RW_KJ] for q in range(RW_KQ))
        s02 = jnp.where(lower_half, b0, b2) + pltpu.roll(jnp.where(lower_half, b2, b0), 2 * quarter, 1)
        s13 = jnp.where(lower_half, b1, b3) + pltpu.roll(jnp.where(lower_half, b3, b1), 2 * quarter, 1)
        return jnp.where(even_quarter, s02 + pltpu.roll(s02, 3 * quarter, 1), s13 + pltpu.roll(s13, quarter, 1))

    def step(t, carry):
        sa_f, part_b = carry
        tb = tblk - 1 - t
        at_end = t == tblk - 1
        sa_b = over_keys(part_b)
        tb_prev = jnp.minimum(tb + 1, tblk - 1)
        yb_ref[tb_prev] = finish_rows(yraw_ref[tb_prev])
        part_f, y_f = one_direction(0, t, jnp.minimum(t + 1, tblk - 1), at_end, af_ref, afn_ref, rf_ref,
                                    vf_ref, zf_ref, sa_f)
        yf_ref[t] = finish_rows(y_f)
        part_b, y_b = one_direction(1, tb, jnp.maximum(tb - 1, 0), at_end, ab_ref, abn_ref, rb_ref, vb_ref,
                                    zb_ref, sa_b)
        yraw_ref[tb] = y_b
        return over_keys(part_f), part_b

    sa_f, part_b = lax.fori_loop(0, tblk, step, (sa_ref[0], sa_ref[1]), unroll=2)
    sa_ref[0] = sa_f
    sa_ref[1] = part_b
    yb_ref[0] = finish_rows(yraw_ref[0])


def rwkv_scan(a_s, r_s, v_s, zf_s, zb_s, ctx_len, tblk=RW_TBLK):
    t = a_s.shape[0]
    nblk = t // tblk
    ncb = ctx_len // tblk
    kh = RW_KJ

    def rev(i):
        return jnp.where(i < ncb, ncb - 1 - i, nblk - 1 + ncb - i)

    def fwd3(i):
        return (i, 0, 0)

    def bwd3(i):
        return (rev(i), 0, 0)

    return pl.pallas_call(
        functools.partial(_rwkv_scan_kernel, tblk=tblk),
        grid=(nblk,),
        in_specs=[pl.BlockSpec((tblk, kh, LANES), fwd3),
                  pl.BlockSpec((1, kh, LANES), lambda i: (jnp.minimum(i + 1, nblk - 1) * tblk, 0, 0)),
                  pl.BlockSpec((tblk, kh, LANES), bwd3),
                  pl.BlockSpec((1, kh, LANES), lambda i: (rev(jnp.minimum(i + 1, nblk - 1)) * tblk + tblk - 1, 0, 0)),
                  pl.BlockSpec((tblk, kh, LANES), fwd3),
                  pl.BlockSpec((tblk, kh, LANES), bwd3),
                  pl.BlockSpec((tblk, HEAD_DIM, LANES), fwd3),
                  pl.BlockSpec((tblk, HEAD_DIM, LANES), bwd3),
                  pl.BlockSpec((tblk, 3, kh, LANES), lambda i: (i, 0, 0, 0)),
                  pl.BlockSpec((tblk, 3, kh, LANES), lambda i: (rev(i), 0, 0, 0))],
        out_specs=[pl.BlockSpec((tblk, kh, LANES), fwd3), pl.BlockSpec((tblk, kh, LANES), bwd3)],
        out_shape=[jax.ShapeDtypeStruct((t, kh, LANES), F32)] * 2,
        scratch_shapes=[pltpu.VMEM((2, kh, HEAD_DIM, LANES), F32), pltpu.VMEM((2, HEAD_DIM, LANES), F32),
                        pltpu.VMEM((tblk, HEAD_DIM, LANES), F32)],
        compiler_params=_cparams("arbitrary"),
        name="rwkv_scan",
    )(a_s, a_s, a_s, a_s, r_s, r_s, v_s, v_s, zf_s, zb_s)


def _rwkv_post_kernel(yf_ref, yb_ref, gb_ref, lw_ref, lb_ref, ones_ref, o_ref):
    w = RW_W
    ones = ones_ref[...]
    y = yf_ref[...] + yb_ref[...]
    yc = y - _dot_f32_mask(y, ones) * (1.0 / HEAD_DIM)
    var = _dot_f32_mask(yc * yc, ones) * (1.0 / HEAD_DIM)
    yn = yc * lax.rsqrt(var + RW_GN_EPS) * lw_ref[...] + lb_ref[...]
    o_ref[...] = ((yn + gb_ref[:, w:]) * gb_ref[:, 0:w]).astype(o_ref.dtype)


def rwkv_post(yf, yb, gb, ln_w, ln_b):
    nt, w = yf.shape
    head = np.arange(w) // HEAD_DIM
    ones = jnp.asarray(head[:, None] == head[None, :], BF16)
    row = lambda n: pl.BlockSpec((TM, n), lambda i: (i, 0))
    const = lambda shape: pl.BlockSpec(shape, lambda i: (0, 0))
    return pl.pallas_call(
        _rwkv_post_kernel,
        grid=(nt // TM,),
        in_specs=[row(w), row(w), row(2 * w), const((1, w)), const((1, w)), const((w, w))],
        out_specs=row(w),
        out_shape=jax.ShapeDtypeStruct((nt, w), BF16),
        compiler_params=_cparams("parallel"),
        name="rwkv_post",
    )(yf, yb, gb, ln_w.reshape(1, w), ln_b.reshape(1, w), ones)


def _rw_keys_to_scan(z, bsz, nq):
    t = z.shape[0] // bsz
    z = z.reshape(bsz, t, nq, RW_HEADS, RW_KQ, RW_KJ)
    z = jnp.transpose(z, (1, 2, 5, 4, 0, 3))
    return z.reshape(t, nq, RW_KJ, LANES)


def _rw_vals_to_scan(v, bsz):
    t = v.shape[0] // bsz
    z = jnp.transpose(v.reshape(bsz, t, RW_HEADS, HEAD_DIM), (1, 3, 0, 2))
    z = jnp.broadcast_to(z[:, :, None], (t, HEAD_DIM, RW_KQ, bsz, RW_HEADS))
    return z.reshape(t, HEAD_DIM, LANES)


def _rw_vals_from_scan(y, bsz):
    t = y.shape[0]
    z = y.reshape(t, RW_KJ, RW_KQ, bsz, RW_HEADS)
    z = jnp.transpose(z, (3, 0, 4, 2, 1))
    return z.reshape(bsz * t, RW_W)


def rwkv_front(ub, bsz, ctx_len, mu, w0, w2, a0, a2, g2, k_k, k_a, r_k):
    tp = ub.shape[0] // bsz
    a_t, r_t, v_t, zf_t, zb_t, gb = rwkv_prep(ub, tp // TM, ctx_len // TM, mu, w0, w2, a0, a2, g2, k_k, k_a, r_k)
    scan_args = (_rw_keys_to_scan(a_t, bsz, 1)[:, 0], _rw_keys_to_scan(r_t, bsz, 1)[:, 0],
                 _rw_vals_to_scan(v_t, bsz), _rw_keys_to_scan(zf_t, bsz, 3), _rw_keys_to_scan(zb_t, bsz, 3))
    return scan_args, gb


def rwkv_back(scan_args, gb, bsz, ctx_len, ln_w, ln_b, tblk=RW_TBLK):
    yf, yb = rwkv_scan(*scan_args, ctx_len, tblk)
    return rwkv_post(_rw_vals_from_scan(yf, bsz), _rw_vals_from_scan(yb, bsz), gb, ln_w, ln_b)


def rwkv_mixer(ub, bsz, ctx_len, mu, w0, w2, a0, a2, g2, k_k, k_a, r_k, ln_w, ln_b, tblk=RW_TBLK):
    scan_args, gb = rwkv_front(ub, bsz, ctx_len, mu, w0, w2, a0, a2, g2, k_k, k_a, r_k)
    return rwkv_back(scan_args, gb, bsz, ctx_len, ln_w, ln_b, tblk)


def _split3(x):
    hi = x.astype(BF16)
    r1 = x - hi.astype(F32)
    mid = r1.astype(BF16)
    return hi, mid, (r1 - mid.astype(F32)).astype(BF16)


def _dot_mask_f32(m, x):
    hi, mid, lo = _split3(x)
    return _dot(m, hi) + _dot(m, mid) + _dot(m, lo)


def _dot_f32_mask(x, m):
    hi, mid, lo = _split3(x)
    return _dot(hi, m) + _dot(mid, m) + _dot(lo, m)


def _ssd_chunk(d, x_ref, bt_ref, c_ref, b_ref, dt_ref, dtt_ref, dth_ref, a_ref, at_ref, ah_ref, y_ref, st_ref):
    q = SSM_CHUNK
    row = lax.broadcasted_iota(jnp.int32, (q, q), 0)
    col = lax.broadcasted_iota(jnp.int32, (q, q), 1)
    mask = row >= col if d == 0 else row <= col
    tri = mask.astype(F32)
    acs = _dot_f32(tri, dt_ref[0, 0] * a_ref[d])
    acs_t = lax.dot_general(dtt_ref[0, 0] * at_ref[d], tri, (((1,), (1,)), ((), ())),
                            precision=lax.Precision.HIGHEST, preferred_element_type=F32)
    dt_h = dth_ref[0]
    dta_h = dt_h * ah_ref[d]
    acs_h = _dot_mask_f32(tri.astype(BF16), dta_h)
    tot_h = jnp.sum(dta_h, axis=0, keepdims=True)
    xdt = x_ref[0] * dt_h
    xdt_bf = xdt.astype(BF16)
    xw_bf = (xdt * jnp.exp(tot_h - acs_h)).astype(BF16)
    eacs_h = jnp.exp(acs_h)
    etot_h = jnp.exp(tot_h)
    gw = SSM_HPG * HEAD_DIM
    pair_lane = lax.broadcasted_iota(jnp.int32, (q, 2 * HEAD_DIM), 1)
    for g in range(SSM_GROUPS):
        gs = slice(g * SSM_STATE, (g + 1) * SSM_STATE)
        gl = slice(g * gw, (g + 1) * gw)
        cm = c_ref[0, :, gs]
        bm = b_ref[0, :, gs]
        bt = bt_ref[0, gs, :]
        cb = _dot_nt(cm, bm)
        state = st_ref[d, g]
        y_off = _dot(cm, state.astype(BF16)) * eacs_h[:, gl]
        for pr in range(SSM_HPG // 2):
            e0 = g * SSM_HPG + 2 * pr
            ps = slice(e0 * HEAD_DIM, (e0 + 2) * HEAD_DIM)
            both = []
            for e in (e0, e0 + 1):
                seg = jnp.exp(jnp.where(mask, acs[:, e:e + 1] - acs_t[e:e + 1, :], MASK_NEG))
                both.append(_dot((cb * seg).astype(BF16), xdt_bf[:, ps]))
            y_diag = jnp.where(pair_lane < HEAD_DIM, both[0], both[1])
            y_ref[0, :, ps] = y_diag + y_off[:, 2 * pr * HEAD_DIM:(2 * pr + 2) * HEAD_DIM]
        st_ref[d, g] = state * etot_h[:, gl] + _dot(bt, xw_bf[:, gl])


def _ssd_kernel(xf_ref, btf_ref, cf_ref, bf_ref, dtf_ref, dttf_ref, dthf_ref,
                xb_ref, btb_ref, cb_ref, bb_ref, dtb_ref, dttb_ref, dthb_ref,
                a_ref, at_ref, ah_ref, yf_ref, yb_ref, st_ref):
    @pl.when(pl.program_id(1) == 0)
    def _init():
        st_ref[...] = jnp.zeros_like(st_ref)

    _ssd_chunk(0, xf_ref, btf_ref, cf_ref, bf_ref, dtf_ref, dttf_ref, dthf_ref, a_ref, at_ref, ah_ref,
               yf_ref, st_ref)
    _ssd_chunk(1, xb_ref, btb_ref, cb_ref, bb_ref, dtb_ref, dttb_ref, dthb_ref, a_ref, at_ref, ah_ref,
               yb_ref, st_ref)


def ssd_scan(xs, bm, cm, dt, dth, a, ctx_len):
    bsz, t, _ = xs.shape
    q = SSM_CHUNK
    nc = t // q
    nctx = ctx_len // q
    bt = jnp.swapaxes(bm, 1, 2)
    dtt = jnp.swapaxes(dt, 2, 3)

    def cidx(d, j):
        return j if d == 0 else jnp.where(j < nctx, nctx - 1 - j, nc - 1 - (j - nctx))

    def specs(d):
        return [pl.BlockSpec((1, q, SSM_W), lambda b, j: (b, cidx(d, j), 0)),
                pl.BlockSpec((1, SSM_GN, q), lambda b, j: (b, 0, cidx(d, j))),
                pl.BlockSpec((1, q, SSM_GN), lambda b, j: (b, cidx(d, j), 0)),
                pl.BlockSpec((1, q, SSM_GN), lambda b, j: (b, cidx(d, j), 0)),
                pl.BlockSpec((1, 1, q, SSM_HEADS), lambda b, j: (b, d, cidx(d, j), 0)),
                pl.BlockSpec((1, 1, SSM_HEADS, q), lambda b, j: (b, d, 0, cidx(d, j))),
                pl.BlockSpec((1, q, SSM_W), lambda b, j: (b, cidx(d, j), d))]

    dir_args = (xs, bt, cm, bm, dt, dtt, dth)
    whole = lambda shape: pl.BlockSpec(shape, lambda b, j: (0, 0, 0))
    return pl.pallas_call(
        _ssd_kernel,
        grid=(bsz, nc),
        in_specs=specs(0) + specs(1) + [whole((2, 1, SSM_HEADS)), whole((2, SSM_HEADS, 1)), whole((2, 1, SSM_W))],
        out_specs=[pl.BlockSpec((1, q, SSM_W), lambda b, j: (b, cidx(0, j), 0)),
                   pl.BlockSpec((1, q, SSM_W), lambda b, j: (b, cidx(1, j), 0))],
        out_shape=[jax.ShapeDtypeStruct((bsz, t, SSM_W), F32)] * 2,
        scratch_shapes=[pltpu.VMEM((2, SSM_GROUPS, SSM_STATE, SSM_HPG * HEAD_DIM), F32)],
        compiler_params=_cparams("parallel", "arbitrary"),
        name="ssd_scan",
    )(*dir_args, *dir_args, a.reshape(2, 1, SSM_HEADS), a.reshape(2, SSM_HEADS, 1),
      jnp.repeat(a, HEAD_DIM, axis=1).reshape(2, 1, SSM_W))


def _ssm_prep_kernel(u_ref, up_ref, un_ref, cw_ref, cb_ref, dtb_ref, spread_ref, xs_out, b_out, c_out, dt_out,
                     dth_out, *, blocks_per_batch, ctx_blocks):
    first, last = _segment_edges(pl.program_id(0), blocks_per_batch, ctx_blocks)
    lo, hi = SSM_W, SSM_W + SSM_XBC
    xe = _with_halo(u_ref[:, lo:hi], up_ref[:, lo:hi], un_ref[:, lo:hi], first, last)
    acc = cb_ref[...] + cw_ref[0:1, :] * _shifted(xe, SSM_CONV // 2)
    for i in range(1, SSM_CONV):
        acc = acc + cw_ref[i:i + 1, :] * _shifted(xe, SSM_CONV // 2 - i)
    xbc = acc * jax.nn.sigmoid(acc)
    xs_out[...] = xbc[:, :SSM_W]
    b_out[...] = xbc[:, SSM_W:SSM_W + SSM_GN].astype(b_out.dtype)
    c_out[...] = xbc[:, SSM_W + SSM_GN:].astype(c_out.dtype)
    dt = _softplus(u_ref[:, hi:] + dtb_ref[...])
    dt_out[...] = dt
    dth_out[...] = _dot_f32_mask(dt, spread_ref[...])


def ssm_prep(uc, blocks_per_batch, ctx_blocks, conv_w, conv_b, dt_bias):
    nt = uc.shape[0]
    npad = SSM_PAD - SSM_W - SSM_XBC
    dtb = jnp.pad(dt_bias.reshape(-1), (0, npad - 2 * SSM_HEADS)).reshape(1, npad)
    spread = np.zeros((npad, 2 * SSM_W), np.float32)
    spread[:2 * SSM_HEADS] = np.repeat(np.eye(2 * SSM_HEADS, dtype=np.float32), HEAD_DIM, axis=1)
    const = lambda shape: pl.BlockSpec(shape, lambda i: (0, 0))
    row = lambda n: pl.BlockSpec((TM, n), lambda i: (i, 0))
    return pl.pallas_call(
        functools.partial(_ssm_prep_kernel, blocks_per_batch=blocks_per_batch, ctx_blocks=ctx_blocks),
        grid=(nt // TM,),
        in_specs=_halo_specs(nt, SSM_PAD) + [const((SSM_CONV, SSM_XBC)), const((1, SSM_XBC)), const((1, npad)),
                                             const((npad, 2 * SSM_W))],
        out_specs=[row(SSM_W), row(SSM_GN), row(SSM_GN), row(npad), row(2 * SSM_W)],
        out_shape=[jax.ShapeDtypeStruct((nt, SSM_W), F32), jax.ShapeDtypeStruct((nt, SSM_GN), BF16),
                   jax.ShapeDtypeStruct((nt, SSM_GN), BF16), jax.ShapeDtypeStruct((nt, npad), F32),
                   jax.ShapeDtypeStruct((nt, 2 * SSM_W), F32)],
        compiler_params=_cparams("parallel"),
        name="ssm_prep",
    )(uc, uc, uc, conv_w, conv_b.reshape(1, SSM_XBC), dtb, jnp.asarray(spread, BF16))


def _ssm_post_kernel(yf_ref, yb_ref, xs_ref, z_ref, d_ref, nw_ref, o_ref):
    z = z_ref[...]
    y = (yf_ref[...] + yb_ref[...] + d_ref[...] * xs_ref[...]) * (z * jax.nn.sigmoid(z))
    gw = SSM_HPG * HEAD_DIM
    for g in range(SSM_GROUPS):
        gs = slice(g * gw, (g + 1) * gw)
        yg = y[:, gs]
        ms = jnp.mean(yg * yg, -1, keepdims=True)
        o_ref[:, gs] = (yg * lax.rsqrt(ms + RMS_EPS) * nw_ref[:, gs]).astype(o_ref.dtype)


def ssm_post(yf, yb, xs, uc, d_skip, norm_w):
    nt = xs.shape[0]
    row = pl.BlockSpec((TM, SSM_W), lambda i: (i, 0))
    const = pl.BlockSpec((1, SSM_W), lambda i: (0, 0))
    return pl.pallas_call(
        _ssm_post_kernel,
        grid=(nt // TM,),
        in_specs=[row, row, row, row, const, const],
        out_specs=row,
        out_shape=jax.ShapeDtypeStruct((nt, SSM_W), BF16),
        compiler_params=_cparams("parallel"),
        name="ssm_post",
    )(yf, yb, xs, uc, jnp.repeat(d_skip, HEAD_DIM).reshape(1, SSM_W), norm_w.reshape(1, SSM_W))


def ssm_mixer(uc, bsz, ctx_len, conv_w, conv_b, dt_bias, a_log, d_skip, norm_w):
    nt = uc.shape[0]
    tp = nt // bsz
    xs, bm, cm, dtp, dth = ssm_prep(uc, tp // TM, ctx_len // TM, conv_w, conv_b, dt_bias)
    dt = jnp.transpose(dtp[:, :2 * SSM_HEADS].reshape(bsz, tp, 2, SSM_HEADS), (0, 2, 1, 3))
    a = -jnp.exp(a_log.astype(F32)).reshape(2, SSM_HEADS)
    yf, yb = ssd_scan(xs.reshape(bsz, tp, SSM_W), bm.reshape(bsz, tp, SSM_GN), cm.reshape(bsz, tp, SSM_GN),
                      dt, dth.reshape(bsz, tp, 2 * SSM_W), a, ctx_len)
    return ssm_post(yf.reshape(nt, SSM_W), yb.reshape(nt, SSM_W), xs, uc, d_skip, norm_w)


def _adaln_router_kernel(x_ref, sh_ref, sc_ref, wr_ref, f_ref, lg_ref):
    xm = x_ref[...] * (1.0 + sc_ref[0]) + sh_ref[0]
    f_ref[...] = xm.astype(f_ref.dtype)
    lg_ref[...] = jnp.dot(xm, wr_ref[...], precision=lax.Precision.HIGHEST, preferred_element_type=F32)


def adaln_router(x, mod_tab, sh_idx, sc_idx, wr):
    nt, d = x.shape
    n = wr.shape[1]
    return pl.pallas_call(
        _adaln_router_kernel,
        grid=(nt // TM,),
        in_specs=[pl.BlockSpec((TM, d), lambda i: (i, 0))] + _mod_specs(d, sh_idx, sc_idx)
        + [pl.BlockSpec((d, n), lambda i: (0, 0))],
        out_specs=[pl.BlockSpec((TM, d), lambda i: (i, 0)), pl.BlockSpec((TM, n), lambda i: (i, 0))],
        out_shape=[jax.ShapeDtypeStruct((nt, d), BF16), jax.ShapeDtypeStruct((nt, n), F32)],
        compiler_params=_cparams("parallel"),
        name="adaln_router",
    )(x, mod_tab, mod_tab, wr)


def _expert_changed(be_ref, i):
    return jnp.logical_or(i == 0, be_ref[i] != be_ref[jnp.maximum(i - 1, 0)])


def _moe_up_kernel(be_ref, x_ref, wg_ref, wu_ref, o_ref, wg_bf, wu_bf):
    @pl.when(_expert_changed(be_ref, pl.program_id(1)))
    def _cast():
        wg_bf[...] = wg_ref[0, 0].astype(BF16)
        wu_bf[...] = wu_ref[0, 0].astype(BF16)

    x = x_ref[...]
    g = _dot(x, wg_bf[...])
    u = _dot(x, wu_bf[...])
    o_ref[...] = (g * jax.nn.sigmoid(g) * u).astype(o_ref.dtype)


def moe_up(xb, block_e, wg, wu, layer, n_split=2):
    ns, d = xb.shape
    f = wg.shape[3]
    tn = f // n_split
    return pl.pallas_call(
        _moe_up_kernel,
        grid_spec=pltpu.PrefetchScalarGridSpec(
            num_scalar_prefetch=1,
            grid=(n_split, ns // MOE_BLOCK),
            in_specs=[pl.BlockSpec((MOE_BLOCK, d), lambda n, i, be: (i, 0)),
                      pl.BlockSpec((1, 1, d, tn), lambda n, i, be: (layer, be[i], 0, n)),
                      pl.BlockSpec((1, 1, d, tn), lambda n, i, be: (layer, be[i], 0, n))],
            out_specs=pl.BlockSpec((MOE_BLOCK, tn), lambda n, i, be: (i, n)),
            scratch_shapes=[pltpu.VMEM((d, tn), BF16), pltpu.VMEM((d, tn), BF16)]),
        out_shape=jax.ShapeDtypeStruct((ns, f), BF16),
        compiler_params=_cparams("arbitrary", "arbitrary"),
        name="moe_up",
    )(block_e, xb, wg, wu)


def _moe_down_kernel(be_ref, a_ref, wd_ref, o_ref, wd_bf):
    @pl.when(_expert_changed(be_ref, pl.program_id(0)))
    def _cast():
        wd_bf[...] = wd_ref[0, 0].astype(BF16)

    o_ref[...] = _dot(a_ref[...], wd_bf[...])


def moe_down(act, block_e, wd, layer):
    ns, f = act.shape
    d = wd.shape[3]
    return pl.pallas_call(
        _moe_down_kernel,
        grid_spec=pltpu.PrefetchScalarGridSpec(
            num_scalar_prefetch=1,
            grid=(ns // MOE_BLOCK,),
            in_specs=[pl.BlockSpec((MOE_BLOCK, f), lambda i, be: (i, 0)),
                      pl.BlockSpec((1, 1, f, d), lambda i, be: (layer, be[i], 0, 0))],
            out_specs=pl.BlockSpec((MOE_BLOCK, d), lambda i, be: (i, 0)),
            scratch_shapes=[pltpu.VMEM((f, d), BF16)]),
        out_shape=jax.ShapeDtypeStruct((ns, d), F32),
        compiler_params=_cparams("arbitrary"),
        name="moe_down",
    )(block_e, act, wd)


def _moe_res_ln_kernel(h_ref, y0_ref, y1_ref, gt_ref, gate_ref, g_ref, b_ref, out_ref):
    gt = gt_ref[...]
    y = y0_ref[...] * gt[:, 0:1] + y1_ref[...] * gt[:, 1:2]
    z = DEEPNORM_ALPHA * h_ref[...] + gate_ref[0] * y
    out_ref[...] = _layer_norm(z, g_ref[...], b_ref[...])


def moe_res_ln(h, y0, y1, gates, mod_tab, gate_idx, ln_g, ln_b):
    nt, d = h.shape
    row = pl.BlockSpec((TM, d), lambda i: (i, 0))
    return pl.pallas_call(
        _moe_res_ln_kernel,
        grid=(nt // TM,),
        in_specs=[row, row, row,
                  pl.BlockSpec((TM, TOP_K), lambda i: (i, 0)),
                  pl.BlockSpec((1, 1, d), lambda i: (i, 0, gate_idx)),
                  pl.BlockSpec((1, d), lambda i: (0, 0)),
                  pl.BlockSpec((1, d), lambda i: (0, 0))],
        out_specs=row,
        out_shape=jax.ShapeDtypeStruct((nt, d), F32),
        compiler_params=_cparams("parallel"),
        name="moe_res_ln",
    )(h, y0, y1, gates, mod_tab, ln_g.reshape(1, d), ln_b.reshape(1, d))


def moe_mixer(h, mod_tab, router_w, router_b, wg, wu, wd, layer, ln_g, ln_b):
    n_tok, d = h.shape
    wr = jnp.pad(router_w, ((0, 0), (0, ROUTER_PAD - N_EXPERTS)))
    f_bf, logits = adaln_router(h, mod_tab, 3, 4, wr)
    logits = logits[:, :N_EXPERTS] + router_b
    top_val, top_idx = lax.top_k(logits, TOP_K)
    gates = jax.nn.softmax(top_val, axis=-1)
    n_assign = n_tok * TOP_K
    flat_e = top_idx.reshape(-1)
    onehot = (flat_e[:, None] == jnp.arange(N_EXPERTS)[None, :]).astype(jnp.int32)
    csum = jnp.cumsum(onehot, axis=0)
    counts = csum[-1]
    rank = jnp.sum((csum - onehot) * onehot, axis=1)
    padded = (counts + MOE_BLOCK - 1) // MOE_BLOCK * MOE_BLOCK
    pad_end = jnp.cumsum(padded)
    pad_start = pad_end - padded
    dest = pad_start[flat_e] + rank
    n_blocks = -(-n_assign // MOE_BLOCK) + N_EXPERTS
    flat_tok = jnp.repeat(jnp.arange(n_tok, dtype=jnp.int32), TOP_K)
    slot_tok = jnp.full((n_blocks * MOE_BLOCK,), n_tok, jnp.int32).at[dest].set(flat_tok)
    block_e = jnp.minimum(jnp.searchsorted(pad_end, jnp.arange(n_blocks) * MOE_BLOCK, side='right'),
                          N_EXPERTS - 1).astype(jnp.int32)
    f_pad = jnp.concatenate([f_bf, jnp.zeros((1, d), f_bf.dtype)], axis=0)
    xb = f_pad[slot_tok]
    act = moe_up(xb, block_e, wg, wu, layer)
    yb = moe_down(act, block_e, wd, layer)
    dest2 = dest.reshape(n_tok, TOP_K)
    return moe_res_ln(h, yb[dest2[:, 0]], yb[dest2[:, 1]], gates, mod_tab, 5, ln_g, ln_b)


def kernel(x, c, ctx, c_ctx, w_mod, b_mod, w_in, w_out, ln1_g, ln1_b, ln2_g, ln2_b, na_rpb, rw_mu, rw_w0, rw_w2, rw_a0, rw_a2, rw_g2, rw_k_k, rw_k_a, rw_r_k, rw_ln_w, rw_ln_b, ssm_conv_w, ssm_conv_b, ssm_dt_bias, ssm_a_log, ssm_d, ssm_norm_w, ffn_w_gate, ffn_w_up, ffn_w_down, moe_router, moe_router_b, moe_w_gate, moe_w_up, moe_w_down):
    bsz, seq, d = x.shape
    ctx_len = ctx.shape[1]
    tp = ctx_len + seq
    nt = bsz * tp
    assert ctx_len % TM == 0 and seq % TM == 0 and RW_KQ * bsz * RW_HEADS == LANES
    rows = seq // GRID_W
    depth = w_mod.shape[0]

    h = jnp.concatenate([ctx, x], axis=1).reshape(nt, d)

    n_mod_rows = -(-(bsz + 1) // 16) * 16
    cin = jnp.concatenate([c, c_ctx[None, :], jnp.zeros((n_mod_rows - bsz - 1, d), F32)], axis=0)
    mod_all = modulation(cin, w_mod, b_mod)
    blk = np.arange(nt // TM)
    blk_row = np.where(blk % (tp // TM) < ctx_len // TM, bsz, blk // (tp // TM))

    for l in range(depth):
        mod_tab = mod_all[l][blk_row][:, None, :]
        wl = w_in[l]
        w_in_p = jnp.concatenate(
            [wl[:, :NA_COLS],
             jnp.pad(wl[:, NA_COLS:NA_COLS + RW_COLS], ((0, 0), (0, RW_PAD - RW_COLS))),
             jnp.pad(wl[:, NA_COLS + RW_COLS:], ((0, 0), (0, SSM_PAD - SSM_COLS)))], axis=1).astype(BF16)
        ua, ub, uc = adaln_proj(h, mod_tab, 0, 1, w_in_p, (NA_COLS, RW_PAD, SSM_PAD), (BF16, F32, F32))
        rw_args, rw_gb = rwkv_front(ub, bsz, ctx_len, rw_mu[l], rw_w0[l], rw_w2[l], rw_a0[l], rw_a2[l],
                                    rw_g2[l], rw_k_k[l], rw_k_a[l], rw_r_k[l])
        oa = na_attention(ua.reshape(bsz, tp, NA_COLS), _na_bias_table(na_rpb[l], rows), ctx_len)
        oc = ssm_mixer(uc, bsz, ctx_len, ssm_conv_w[l], ssm_conv_b[l], ssm_dt_bias[l], ssm_a_log[l],
                       ssm_d[l], ssm_norm_w[l])
        ob = rwkv_back(rw_args, rw_gb, bsz, ctx_len, rw_ln_w[l], rw_ln_b[l])
        h = mix_proj_res_ln(oa.reshape(nt, NA_W), ob, oc, w_out[l].astype(BF16), h, mod_tab, 2,
                            ln1_g[l], ln1_b[l])

        j = l // 2
        if l % 2 == 0:
            act = adaln_swiglu(h, mod_tab, 3, 4, ffn_w_gate[j].astype(BF16), ffn_w_up[j].astype(BF16))
            h = proj_res_ln(act, ffn_w_down[j].astype(BF16), h, mod_tab, 5, ln2_g[l], ln2_b[l])
        else:
            h = moe_mixer(h, mod_tab, moe_router[j], moe_router_b[j], moe_w_gate, moe_w_up, moe_w_down, j,
                          ln2_g[l], ln2_b[l])
    return h.reshape(bsz, tp, d)[:, ctx_len:]
```

```python
import functools

import numpy as np
import jax
import jax.numpy as jnp
from jax import lax
from jax.experimental import pallas as pl
from jax.experimental.pallas import tpu as pltpu

F32 = jnp.float32
BF16 = jnp.bfloat16

D_MODEL = 1024
DEPTH = 4
GRID_W = 64
HEAD_DIM = 64
NA_HEADS = 4
RW_HEADS = 4
SSM_HEADS = 8
NA_W = NA_HEADS * HEAD_DIM
RW_W = RW_HEADS * HEAD_DIM
SSM_W = SSM_HEADS * HEAD_DIM
NA_WIN_R = 8
NA_WIN_C = 16
RW_DECAY_LORA = 32
RW_AAA_LORA = 32
RW_GATE_LORA = 64
RW_GN_EPS = 64e-5
SSM_GROUPS = 2
SSM_HPG = SSM_HEADS // SSM_GROUPS
SSM_STATE = 128
SSM_CONV = 5
SSM_CHUNK = 128
SSM_GN = SSM_GROUPS * SSM_STATE
SSM_XBC = SSM_W + 2 * SSM_GN
NA_COLS = 3 * NA_W
RW_COLS = 3 * RW_W + 2 * RW_DECAY_LORA + 2 * RW_AAA_LORA + RW_GATE_LORA
SSM_COLS = SSM_W + SSM_XBC + 2 * SSM_HEADS
N_EXPERTS = 8
TOP_K = 2
MOE_BLOCK = 512
DEEPNORM_ALPHA = (2 * DEPTH) ** 0.25
LN_EPS = 1e-5
RMS_EPS = 1e-5

LANES = 128
TM = 256
RW_PAD = 1024
SSM_PAD = 1664
ROUTER_PAD = LANES
VMEM_LIMIT = 56 * 1024 * 1024
MASK_NEG = -1e30


def _cparams(*sem):
    return pltpu.CompilerParams(dimension_semantics=sem, vmem_limit_bytes=VMEM_LIMIT)


def _dot(a, b):
    return jnp.dot(a, b, preferred_element_type=F32)


def _dot_nt(a, b):
    return lax.dot_general(a, b, (((1,), (1,)), ((), ())), preferred_element_type=F32)


def _layer_norm(z, g, b):
    mu = jnp.mean(z, -1, keepdims=True)
    zc = z - mu
    var = jnp.mean(zc * zc, -1, keepdims=True)
    return zc * lax.rsqrt(var + LN_EPS) * g + b


def _mod_kernel(c_ref, w_ref, b_ref, o_ref):
    cs = c_ref[...]
    cs = cs * jax.nn.sigmoid(cs)
    o_ref[0] = _dot(cs.astype(BF16), w_ref[0].astype(BF16)) + b_ref[0]


def modulation(cin, w_mod, b_mod):
    r, d = cin.shape
    nl, _, n = w_mod.shape
    tn = d
    return pl.pallas_call(
        _mod_kernel,
        grid=(nl, n // tn),
        in_specs=[pl.BlockSpec((r, d), lambda l, j: (0, 0)),
                  pl.BlockSpec((1, d, tn), lambda l, j: (l, 0, j)),
                  pl.BlockSpec((1, 1, tn), lambda l, j: (l, 0, j))],
        out_specs=pl.BlockSpec((1, r, tn), lambda l, j: (l, 0, j)),
        out_shape=jax.ShapeDtypeStruct((nl, r, n), F32),
        compiler_params=_cparams("parallel", "parallel"),
        name="modulation",
    )(cin, w_mod, b_mod.reshape(nl, 1, n))


def _mod_specs(d, idx_a, idx_b):
    return [pl.BlockSpec((1, 1, d), lambda i: (i, 0, idx_a)),
            pl.BlockSpec((1, 1, d), lambda i: (i, 0, idx_b))]


def _adaln_proj_kernel(x_ref, sh_ref, sc_ref, w_ref, *o_refs, splits):
    xm = x_ref[...] * (1.0 + sc_ref[0]) + sh_ref[0]
    y = _dot(xm.astype(BF16), w_ref[...])
    off = 0
    for o_ref, n in zip(o_refs, splits):
        o_ref[...] = y[:, off:off + n].astype(o_ref.dtype)
        off += n


def adaln_proj(x, mod_tab, sh_idx, sc_idx, w, splits, dtypes):
    nt, d = x.shape
    n = w.shape[1]
    return pl.pallas_call(
        functools.partial(_adaln_proj_kernel, splits=splits),
        grid=(nt // TM,),
        in_specs=[pl.BlockSpec((TM, d), lambda i: (i, 0))] + _mod_specs(d, sh_idx, sc_idx)
        + [pl.BlockSpec((d, n), lambda i: (0, 0))],
        out_specs=[pl.BlockSpec((TM, s), lambda i: (i, 0)) for s in splits],
        out_shape=[jax.ShapeDtypeStruct((nt, s), dt) for s, dt in zip(splits, dtypes)],
        compiler_params=_cparams("parallel"),
        name="adaln_proj",
    )(x, mod_tab, mod_tab, w)


def _adaln_swiglu_kernel(x_ref, sh_ref, sc_ref, wg_ref, wu_ref, o_ref):
    xm = (x_ref[...] * (1.0 + sc_ref[0]) + sh_ref[0]).astype(BF16)
    g = _dot(xm, wg_ref[...])
    u = _dot(xm, wu_ref[...])
    o_ref[...] = (g * jax.nn.sigmoid(g) * u).astype(o_ref.dtype)


def adaln_swiglu(x, mod_tab, sh_idx, sc_idx, wg, wu):
    nt, d = x.shape
    f = wg.shape[1]
    return pl.pallas_call(
        _adaln_swiglu_kernel,
        grid=(nt // TM,),
        in_specs=[pl.BlockSpec((TM, d), lambda i: (i, 0))] + _mod_specs(d, sh_idx, sc_idx)
        + [pl.BlockSpec((d, f), lambda i: (0, 0)), pl.BlockSpec((d, f), lambda i: (0, 0))],
        out_specs=pl.BlockSpec((TM, f), lambda i: (i, 0)),
        out_shape=jax.ShapeDtypeStruct((nt, f), BF16),
        compiler_params=_cparams("parallel"),
        name="adaln_swiglu",
    )(x, mod_tab, mod_tab, wg, wu)


def _proj_res_ln_kernel(o_ref, w_ref, h_ref, gate_ref, g_ref, b_ref, out_ref):
    y = _dot(o_ref[...], w_ref[...])
    z = DEEPNORM_ALPHA * h_ref[...] + gate_ref[0] * y
    out_ref[...] = _layer_norm(z, g_ref[...], b_ref[...])


def proj_res_ln(o, w, h, mod_tab, gate_idx, ln_g, ln_b):
    nt, k = o.shape
    d = w.shape[1]
    return pl.pallas_call(
        _proj_res_ln_kernel,
        grid=(nt // TM,),
        in_specs=[pl.BlockSpec((TM, k), lambda i: (i, 0)),
                  pl.BlockSpec((k, d), lambda i: (0, 0)),
                  pl.BlockSpec((TM, d), lambda i: (i, 0)),
                  pl.BlockSpec((1, 1, d), lambda i: (i, 0, gate_idx)),
                  pl.BlockSpec((1, d), lambda i: (0, 0)),
                  pl.BlockSpec((1, d), lambda i: (0, 0))],
        out_specs=pl.BlockSpec((TM, d), lambda i: (i, 0)),
        out_shape=jax.ShapeDtypeStruct((nt, d), F32),
        compiler_params=_cparams("parallel"),
        name="proj_res_ln",
    )(o, w, h, mod_tab, ln_g.reshape(1, d), ln_b.reshape(1, d))


def _mix_proj_res_ln_kernel(oa_ref, ob_ref, oc_ref, w_ref, h_ref, gate_ref, g_ref, b_ref, out_ref):
    s1, s2 = NA_W, NA_W + RW_W
    y = (_dot(oa_ref[...], w_ref[0:s1, :]) + _dot(ob_ref[...], w_ref[s1:s2, :])
         + _dot(oc_ref[...], w_ref[s2:, :]))
    z = DEEPNORM_ALPHA * h_ref[...] + gate_ref[0] * y
    out_ref[...] = _layer_norm(z, g_ref[...], b_ref[...])


def mix_proj_res_ln(oa, ob, oc, w, h, mod_tab, gate_idx, ln_g, ln_b):
    nt, d = h.shape
    row = lambda n: pl.BlockSpec((TM, n), lambda i: (i, 0))
    return pl.pallas_call(
        _mix_proj_res_ln_kernel,
        grid=(nt // TM,),
        in_specs=[row(NA_W), row(RW_W), row(SSM_W),
                  pl.BlockSpec((d, d), lambda i: (0, 0)),
                  row(d),
                  pl.BlockSpec((1, 1, d), lambda i: (i, 0, gate_idx)),
                  pl.BlockSpec((1, d), lambda i: (0, 0)),
                  pl.BlockSpec((1, d), lambda i: (0, 0))],
        out_specs=row(d),
        out_shape=jax.ShapeDtypeStruct((nt, d), F32),
        compiler_params=_cparams("parallel"),
        name="mix_proj_res_ln",
    )(oa, ob, oc, w, h, mod_tab, ln_g.reshape(1, d), ln_b.reshape(1, d))


def _na_bias_table(rpb, rows):
    win_r = min(NA_WIN_R, rows)
    q = np.arange(GRID_W)
    col_start = np.clip(q - NA_WIN_C // 2, 0, GRID_W - NA_WIN_C)
    kc = np.arange(GRID_W)
    valid = (kc[None, :] >= col_start[:, None]) & (kc[None, :] < col_start[:, None] + NA_WIN_C)
    crel = np.clip(kc[None, :] - q[:, None] + NA_WIN_C - 1, 0, 2 * NA_WIN_C - 2)
    case = np.arange(NA_WIN_R)
    rrel = np.clip(NA_WIN_R - 1 - case[:, None] + np.arange(win_r)[None, :], 0, 2 * NA_WIN_R - 2)
    t = rpb[:, rrel]
    t = t[:, :, :, crel]
    t = jnp.transpose(t, (1, 0, 3, 2, 4))
    t = jnp.where(valid[None, None, :, None, :], t, MASK_NEG)
    return t.reshape(NA_WIN_R, NA_HEADS * GRID_W, win_r * GRID_W).astype(F32)


NA_ROWS_PER_STEP = 4


def _attend(q, keysets, scale):
    head = lax.broadcasted_iota(jnp.int32, (GRID_W, NA_W), 1) // HEAD_DIM
    zero = jnp.zeros_like(q)
    qs = jnp.concatenate([jnp.where(head == h, q, zero) for h in range(NA_HEADS)], axis=0)
    scores = []
    for keys, _, bias in keysets:
        s = _dot_nt(qs, keys) * scale
        scores.append(s if bias is None else s + bias)
    m = scores[0].max(-1, keepdims=True)
    for s in scores[1:]:
        m = jnp.maximum(m, s.max(-1, keepdims=True))
    den = 0.0
    acc = 0.0
    for s, (_, vals, _) in zip(scores, keysets):
        p = jnp.exp(s - m)
        den = den + p.sum(-1, keepdims=True)
        acc = acc + _dot(p.astype(BF16), vals)
    acc = acc / den
    out = jnp.where(head == 0, acc[0:GRID_W], 0.0)
    for h in range(1, NA_HEADS):
        out = out + jnp.where(head == h, acc[h * GRID_W:(h + 1) * GRID_W], 0.0)
    return out


def _na_kernel(u_ref, bias_ref, o_ref, *, ctx_len, rows, scale):
    j = pl.program_id(1)
    rps = NA_ROWS_PER_STEP
    n_ctx_steps = ctx_len // (rps * GRID_W)
    win_r = min(NA_WIN_R, rows)
    kc = u_ref[0, 0:ctx_len, NA_W:2 * NA_W]
    vc = u_ref[0, 0:ctx_len, 2 * NA_W:3 * NA_W]

    @pl.when(j < n_ctx_steps)
    def _ctx():
        for i in range(rps):
            q0 = pl.multiple_of((j * rps + i) * GRID_W, GRID_W)
            q = u_ref[0, pl.ds(q0, GRID_W), 0:NA_W]
            o_ref[0, i * GRID_W:(i + 1) * GRID_W, :] = _attend(q, [(kc, vc, None)], scale).astype(o_ref.dtype)

    @pl.when(j >= n_ctx_steps)
    def _lat():
        for i in range(rps):
            r = (j - n_ctx_steps) * rps + i
            r0 = jnp.clip(r - win_r // 2, 0, rows - win_r)
            q = u_ref[0, pl.ds(pl.multiple_of(ctx_len + r * GRID_W, GRID_W), GRID_W), 0:NA_W]
            kstart = pl.multiple_of(ctx_len + r0 * GRID_W, GRID_W)
            kl = u_ref[0, pl.ds(kstart, win_r * GRID_W), NA_W:2 * NA_W]
            vl = u_ref[0, pl.ds(kstart, win_r * GRID_W), 2 * NA_W:3 * NA_W]
            out = _attend(q, [(kl, vl, bias_ref[r - r0]), (kc, vc, None)], scale)
            o_ref[0, i * GRID_W:(i + 1) * GRID_W, :] = out.astype(o_ref.dtype)


def na_attention(ua, bias_tab, ctx_len):
    bsz, tp, _ = ua.shape
    rows = (tp - ctx_len) // GRID_W
    qn = NA_ROWS_PER_STEP * GRID_W
    assert ctx_len % qn == 0 and rows % NA_ROWS_PER_STEP == 0
    return pl.pallas_call(
        functools.partial(_na_kernel, ctx_len=ctx_len, rows=rows, scale=HEAD_DIM ** -0.5),
        grid=(bsz, tp // qn),
        in_specs=[pl.BlockSpec((1, tp, NA_COLS), lambda b, j: (b, 0, 0)),
                  pl.BlockSpec(bias_tab.shape, lambda b, j: (0, 0, 0))],
        out_specs=pl.BlockSpec((1, qn, NA_W), lambda b, j: (b, j, 0)),
        out_shape=jax.ShapeDtypeStruct((bsz, tp, NA_W), BF16),
        compiler_params=_cparams("parallel", "arbitrary"),
        name="na_attention",
    )(ua, bias_tab)


RW_KQ = 4
RW_KJ = HEAD_DIM // RW_KQ
RW_TBLK = 64
HALO = 8


def _segment_edges(i, blocks_per_batch, ctx_blocks):
    j = i % blocks_per_batch
    first = jnp.logical_or(j == 0, j == ctx_blocks)
    last = jnp.logical_or(j == ctx_blocks - 1, j == blocks_per_batch - 1)
    return first, last


def _with_halo(x, prev, nxt, first, last):
    return jnp.concatenate([jnp.where(first, 0.0, prev), x, jnp.where(last, 0.0, nxt)], axis=0)


def _shifted(xe, s):
    return pltpu.roll(xe, s % xe.shape[0], 0)[HALO:HALO + TM]


def _halo_specs(nt, width):
    return [pl.BlockSpec((TM, width), lambda i: (i, 0)),
            pl.BlockSpec((HALO, width), lambda i: (jnp.maximum(i * (TM // HALO) - 1, 0), 0)),
            pl.BlockSpec((HALO, width), lambda i: (jnp.minimum((i + 1) * (TM // HALO), nt // HALO - 1), 0))]


def _softplus(x):
    return jnp.maximum(x, 0.0) + jnp.log1p(jnp.exp(-jnp.abs(x)))


def _dot_f32(a, b):
    return jnp.dot(a, b, precision=lax.Precision.HIGHEST, preferred_element_type=F32)


def _rwkv_prep_kernel(u_ref, up_ref, un_ref, mu_ref, lw_ref, g2_ref, w0_ref, a0_ref, kk_ref, ka_ref, rk_ref,
                      ones_ref, a_out, r_out, v_out, zf_out, zb_out, gb_out, *, blocks_per_batch, ctx_blocks):
    first, last = _segment_edges(pl.program_id(0), blocks_per_batch, ctx_blocks)
    xe = _with_halo(u_ref[...], up_ref[...], un_ref[...], first, last)
    x = u_ref[...]
    um = x + mu_ref[...] * (0.5 * (_shifted(xe, 1) + _shifted(xe, -1)) - x)
    w = RW_W
    r, k, v = um[:, 0:w], um[:, w:2 * w], um[:, 2 * w:3 * w]
    lo = um[:, 3 * w:3 * w + LANES]
    lane = lax.broadcasted_iota(jnp.int32, lo.shape, 1)
    lo = jnp.where(lane < 2 * RW_DECAY_LORA, jnp.tanh(lo), lo)
    lora = _dot_f32(lo, lw_ref[...])
    g = _dot_f32(jax.nn.sigmoid(um[:, 3 * w + LANES:3 * w + 2 * LANES]), g2_ref[...])
    wlog = -_softplus(-(w0_ref[...] + lora[:, :2 * w])) - 0.5
    decay = jnp.exp(-jnp.exp(wlog))
    aa = jax.nn.sigmoid(a0_ref[...] + lora[:, 2 * w:])
    ones = ones_ref[...]
    kk = k * kk_ref[...]
    kk = kk * lax.rsqrt(_dot_f32_mask(kk * kk, ones) + 1e-12)
    ka = ka_ref[...]
    kd_f = k * (1.0 + (aa[:, :w] - 1.0) * ka)
    kd_b = k * (1.0 + (aa[:, w:] - 1.0) * ka)
    a_out[...] = -kk
    r_out[...] = r
    v_out[...] = v
    zf_out[:, 0:w] = decay[:, :w]
    zf_out[:, w:2 * w] = kk * aa[:, :w]
    zf_out[:, 2 * w:] = kd_f
    zb_out[:, 0:w] = decay[:, w:]
    zb_out[:, w:2 * w] = kk * aa[:, w:]
    zb_out[:, 2 * w:] = kd_b
    gb_out[:, 0:w] = g
    gb_out[:, w:] = _dot_f32_mask(r * (kd_f + kd_b) * rk_ref[...], ones) * v


def rwkv_prep(ub, blocks_per_batch, ctx_blocks, mu, w0, w2, a0, a2, g2, k_k, k_a, r_k):
    nt = ub.shape[0]
    w = RW_W
    lw = jnp.zeros((LANES, 4 * w), F32)
    lw = lw.at[0:32, 0:w].set(w2[0]).at[32:64, w:2 * w].set(w2[1])
    lw = lw.at[64:96, 2 * w:3 * w].set(a2[0]).at[96:128, 3 * w:].set(a2[1])
    g2p = jnp.pad(g2, ((0, LANES - RW_GATE_LORA), (0, 0)))
    mup = jnp.pad(mu, (0, RW_PAD - RW_COLS)).reshape(1, RW_PAD)
    head = np.arange(w) // HEAD_DIM
    ones = jnp.asarray(head[:, None] == head[None, :], BF16)
    const = lambda shape: pl.BlockSpec(shape, lambda i: (0, 0))
    row = lambda n: pl.BlockSpec((TM, n), lambda i: (i, 0))
    return pl.pallas_call(
        functools.partial(_rwkv_prep_kernel, blocks_per_batch=blocks_per_batch, ctx_blocks=ctx_blocks),
        grid=(nt // TM,),
        in_specs=_halo_specs(nt, RW_PAD) + [const((1, RW_PAD)), const((LANES, 4 * w)), const((LANES, w)),
                                            const((1, 2 * w)), const((1, 2 * w)), const((1, w)), const((1, w)),
                                            const((1, w)), const((w, w))],
        out_specs=[row(w), row(w), row(w), row(3 * w), row(3 * w), row(2 * w)],
        out_shape=[jax.ShapeDtypeStruct((nt, n), F32) for n in (w, w, w, 3 * w, 3 * w, 2 * w)],
        compiler_params=_cparams("parallel"),
        name="rwkv_prep",
    )(ub, ub, ub, mup, lw, g2p, w0.reshape(1, 2 * w), a0.reshape(1, 2 * w), k_k.reshape(1, w),
      k_a.reshape(1, w), r_k.reshape(1, w), ones)


def _rwkv_scan_kernel(af_ref, afn_ref, ab_ref, abn_ref, rf_ref, rb_ref, vf_ref, vb_ref, zf_ref, zb_ref,
                      yf_ref, yb_ref, s_ref, sa_ref, yraw_ref, *, tblk):
    @pl.when(pl.program_id(0) == 0)
    def _init():
        s_ref[...] = jnp.zeros_like(s_ref)
        sa_ref[...] = jnp.zeros_like(sa_ref)
        yraw_ref[...] = jnp.zeros_like(yraw_ref)

    quarter = LANES // RW_KQ

    def over_keys(p):
        ax = p.ndim - 1
        return (p + pltpu.roll(p, quarter, ax) + pltpu.roll(p, 2 * quarter, ax)
                + pltpu.roll(p, 3 * quarter, ax))

    def rows(ref_row):
        return jnp.broadcast_to(ref_row, (HEAD_DIM, LANES))

    def one_direction(d, tc, tn, at_end, a_ref, an_ref, r_ref, v_ref, z_ref, sa):
        v_t = v_ref[tc]
        y = jnp.zeros((HEAD_DIM, LANES), F32)
        sa_next = jnp.zeros((HEAD_DIM, LANES), F32)
        for j in range(RW_KJ):
            js = slice(j, j + 1)
            s_new = (s_ref[d, j] * rows(z_ref[tc, 0, js, :]) + sa * rows(z_ref[tc, 1, js, :])
                     + v_t * rows(z_ref[tc, 2, js, :]))
            s_ref[d, j] = s_new
            y = y + s_new * rows(r_ref[tc, js, :])
            sa_next = sa_next + s_new * rows(jnp.where(at_end, an_ref[0, js, :], a_ref[tn, js, :]))
        return sa_next, y

    lane_y = lax.broadcasted_iota(jnp.int32, (RW_KJ, LANES), 1)
    lower_half = lane_y < 2 * quarter
    even_quarter = (lane_y & quarter) == 0

    def finish_rows(y):
        b0, b1, b2, b3 = (y[q * RW_KJ:(q + 1) * RW_KJ] for q in range(RW_KQ))
        s02 = jnp.where(lower_half, b0, b2) + pltpu.roll(jnp.where(lower_half, b2, b0), 2 * quarter, 1)
        s13 = jnp.where(lower_half, b1, b3) + pltpu.roll(jnp.where(lower_half, b3, b1), 2 * quarter, 1)
        return jnp.where(even_quarter, s02 + pltpu.roll(s02, 3 * quarter, 1), s13 + pltpu.roll(s13, quarter, 1))

    def step(t, carry):
        sa_f, part_b = carry
        tb = tblk - 1 - t
        at_end = t == tblk - 1
        sa_b = over_keys(part_b)
        tb_prev = jnp.minimum(tb + 1, tblk - 1)
        yb_ref[tb_prev] = finish_rows(yraw_ref[tb_prev])
        part_f, y_f = one_direction(0, t, jnp.minimum(t + 1, tblk - 1), at_end, af_ref, afn_ref, rf_ref,
                                    vf_ref, zf_ref, sa_f)
        yf_ref[t] = finish_rows(y_f)
        part_b, y_b = one_direction(1, tb, jnp.maximum(tb - 1, 0), at_end, ab_ref, abn_ref, rb_ref, vb_ref,
                                    zb_ref, sa_b)
        yraw_ref[tb] = y_b
        return over_keys(part_f), part_b

    sa_f, part_b = lax.fori_loop(0, tblk, step, (sa_ref[0], sa_ref[1]), unroll=2)
    sa_ref[0] = sa_f
    sa_ref[1] = part_b
    yb_ref[0] = finish_rows(yraw_ref[0])


def rwkv_scan(a_s, r_s, v_s, zf_s, zb_s, ctx_len, tblk=RW_TBLK):
    t = a_s.shape[0]
    nblk = t // tblk
    ncb = ctx_len // tblk
    kh = RW_KJ

    def rev(i):
        return jnp.where(i < ncb, ncb - 1 - i, nblk - 1 + ncb - i)

    def fwd3(i):
        return (i, 0, 0)

    def bwd3(i):
        return (rev(i), 0, 0)

    return pl.pallas_call(
        functools.partial(_rwkv_scan_kernel, tblk=tblk),
        grid=(nblk,),
        in_specs=[pl.BlockSpec((tblk, kh, LANES), fwd3),
                  pl.BlockSpec((1, kh, LANES), lambda i: (jnp.minimum(i + 1, nblk - 1) * tblk, 0, 0)),
                  pl.BlockSpec((tblk, kh, LANES), bwd3),
                  pl.BlockSpec((1, kh, LANES), lambda i: (rev(jnp.minimum(i + 1, nblk - 1)) * tblk + tblk - 1, 0, 0)),
                  pl.BlockSpec((tblk, kh, LANES), fwd3),
                  pl.BlockSpec((tblk, kh, LANES), bwd3),
                  pl.BlockSpec((tblk, HEAD_DIM, LANES), fwd3),
                  pl.BlockSpec((tblk, HEAD_DIM, LANES), bwd3),
                  pl.BlockSpec((tblk, 3, kh, LANES), lambda i: (i, 0, 0, 0)),
                  pl.BlockSpec((tblk, 3, kh, LANES), lambda i: (rev(i), 0, 0, 0))],
        out_specs=[pl.BlockSpec((tblk, kh, LANES), fwd3), pl.BlockSpec((tblk, kh, LANES), bwd3)],
        out_shape=[jax.ShapeDtypeStruct((t, kh, LANES), F32)] * 2,
        scratch_shapes=[pltpu.VMEM((2, kh, HEAD_DIM, LANES), F32), pltpu.VMEM((2, HEAD_DIM, LANES), F32),
                        pltpu.VMEM((tblk, HEAD_DIM, LANES), F32)],
        compiler_params=_cparams("arbitrary"),
        name="rwkv_scan",
    )(a_s, a_s, a_s, a_s, r_s, r_s, v_s, v_s, zf_s, zb_s)


def _rwkv_post_kernel(yf_ref, yb_ref, gb_ref, lw_ref, lb_ref, ones_ref, o_ref):
    w = RW_W
    ones = ones_ref[...]
    y = yf_ref[...] + yb_ref[...]
    yc = y - _dot_f32_mask(y, ones) * (1.0 / HEAD_DIM)
    var = _dot_f32_mask(yc * yc, ones) * (1.0 / HEAD_DIM)
    yn = yc * lax.rsqrt(var + RW_GN_EPS) * lw_ref[...] + lb_ref[...]
    o_ref[...] = ((yn + gb_ref[:, w:]) * gb_ref[:, 0:w]).astype(o_ref.dtype)


def rwkv_post(yf, yb, gb, ln_w, ln_b):
    nt, w = yf.shape
    head = np.arange(w) // HEAD_DIM
    ones = jnp.asarray(head[:, None] == head[None, :], BF16)
    row = lambda n: pl.BlockSpec((TM, n), lambda i: (i, 0))
    const = lambda shape: pl.BlockSpec(shape, lambda i: (0, 0))
    return pl.pallas_call(
        _rwkv_post_kernel,
        grid=(nt // TM,),
        in_specs=[row(w), row(w), row(2 * w), const((1, w)), const((1, w)), const((w, w))],
        out_specs=row(w),
        out_shape=jax.ShapeDtypeStruct((nt, w), BF16),
        compiler_params=_cparams("parallel"),
        name="rwkv_post",
    )(yf, yb, gb, ln_w.reshape(1, w), ln_b.reshape(1, w), ones)


def _rw_keys_to_scan(z, bsz, nq):
    t = z.shape[0] // bsz
    z = z.reshape(bsz, t, nq, RW_HEADS, RW_KQ, RW_KJ)
    z = jnp.transpose(z, (1, 2, 5, 4, 0, 3))
    return z.reshape(t, nq, RW_KJ, LANES)


def _rw_vals_to_scan(v, bsz):
    t = v.shape[0] // bsz
    z = jnp.transpose(v.reshape(bsz, t, RW_HEADS, HEAD_DIM), (1, 3, 0, 2))
    z = jnp.broadcast_to(z[:, :, None], (t, HEAD_DIM, RW_KQ, bsz, RW_HEADS))
    return z.reshape(t, HEAD_DIM, LANES)


def _rw_vals_from_scan(y, bsz):
    t = y.shape[0]
    z = y.reshape(t, RW_KJ, RW_KQ, bsz, RW_HEADS)
    z = jnp.transpose(z, (3, 0, 4, 2, 1))
    return z.reshape(bsz * t, RW_W)


def rwkv_front(ub, bsz, ctx_len, mu, w0, w2, a0, a2, g2, k_k, k_a, r_k):
    tp = ub.shape[0] // bsz
    a_t, r_t, v_t, zf_t, zb_t, gb = rwkv_prep(ub, tp // TM, ctx_len // TM, mu, w0, w2, a0, a2, g2, k_k, k_a, r_k)
    scan_args = (_rw_keys_to_scan(a_t, bsz, 1)[:, 0], _rw_keys_to_scan(r_t, bsz, 1)[:, 0],
                 _rw_vals_to_scan(v_t, bsz), _rw_keys_to_scan(zf_t, bsz, 3), _rw_keys_to_scan(zb_t, bsz, 3))
    return scan_args, gb


def rwkv_back(scan_args, gb, bsz, ctx_len, ln_w, ln_b, tblk=RW_TBLK):
    yf, yb = rwkv_scan(*scan_args, ctx_len, tblk)
    return rwkv_post(_rw_vals_from_scan(yf, bsz), _rw_vals_from_scan(yb, bsz), gb, ln_w, ln_b)


def rwkv_mixer(ub, bsz, ctx_len, mu, w0, w2, a0, a2, g2, k_k, k_a, r_k, ln_w, ln_b, tblk=RW_TBLK):
    scan_args, gb = rwkv_front(ub, bsz, ctx_len, mu, w0, w2, a0, a2, g2, k_k, k_a, r_k)
    return rwkv_back(scan_args, gb, bsz, ctx_len, ln_w, ln_b, tblk)


def _split3(x):
    hi = x.astype(BF16)
    r1 = x - hi.astype(F32)
    mid = r1.astype(BF16)
    return hi, mid, (r1 - mid.astype(F32)).astype(BF16)


def _dot_mask_f32(m, x):
    hi, mid, lo = _split3(x)
    return _dot(m, hi) + _dot(m, mid) + _dot(m, lo)


def _dot_f32_mask(x, m):
    hi, mid, lo = _split3(x)
    return _dot(hi, m) + _dot(mid, m) + _dot(lo, m)


def _ssd_chunk(d, x_ref, bt_ref, c_ref, b_ref, dt_ref, dtt_ref, dth_ref, a_ref, at_ref, ah_ref, y_ref, st_ref):
    q = SSM_CHUNK
    row = lax.broadcasted_iota(jnp.int32, (q, q), 0)
    col = lax.broadcasted_iota(jnp.int32, (q, q), 1)
    mask = row >= col if d == 0 else row <= col
    tri = mask.astype(F32)
    acs = _dot_f32(tri, dt_ref[0, 0] * a_ref[d])
    acs_t = lax.dot_general(dtt_ref[0, 0] * at_ref[d], tri, (((1,), (1,)), ((), ())),
                            precision=lax.Precision.HIGHEST, preferred_element_type=F32)
    dt_h = dth_ref[0]
    dta_h = dt_h * ah_ref[d]
    acs_h = _dot_mask_f32(tri.astype(BF16), dta_h)
    tot_h = jnp.sum(dta_h, axis=0, keepdims=True)
    xdt = x_ref[0] * dt_h
    xdt_bf = xdt.astype(BF16)
    xw_bf = (xdt * jnp.exp(tot_h - acs_h)).astype(BF16)
    eacs_h = jnp.exp(acs_h)
    etot_h = jnp.exp(tot_h)
    gw = SSM_HPG * HEAD_DIM
    pair_lane = lax.broadcasted_iota(jnp.int32, (q, 2 * HEAD_DIM), 1)
    for g in range(SSM_GROUPS):
        gs = slice(g * SSM_STATE, (g + 1) * SSM_STATE)
        gl = slice(g * gw, (g + 1) * gw)
        cm = c_ref[0, :, gs]
        bm = b_ref[0, :, gs]
        bt = bt_ref[0, gs, :]
        cb = _dot_nt(cm, bm)
        state = st_ref[d, g]
        y_off = _dot(cm, state.astype(BF16)) * eacs_h[:, gl]
        for pr in range(SSM_HPG // 2):
            e0 = g * SSM_HPG + 2 * pr
            ps = slice(e0 * HEAD_DIM, (e0 + 2) * HEAD_DIM)
            both = []
            for e in (e0, e0 + 1):
                seg = jnp.exp(jnp.where(mask, acs[:, e:e + 1] - acs_t[e:e + 1, :], MASK_NEG))
                both.append(_dot((cb * seg).astype(BF16), xdt_bf[:, ps]))
            y_diag = jnp.where(pair_lane < HEAD_DIM, both[0], both[1])
            y_ref[0, :, ps] = y_diag + y_off[:, 2 * pr * HEAD_DIM:(2 * pr + 2) * HEAD_DIM]
        st_ref[d, g] = state * etot_h[:, gl] + _dot(bt, xw_bf[:, gl])


def _ssd_kernel(xf_ref, btf_ref, cf_ref, bf_ref, dtf_ref, dttf_ref, dthf_ref,
                xb_ref, btb_ref, cb_ref, bb_ref, dtb_ref, dttb_ref, dthb_ref,
                a_ref, at_ref, ah_ref, yf_ref, yb_ref, st_ref):
    @pl.when(pl.program_id(1) == 0)
    def _init():
        st_ref[...] = jnp.zeros_like(st_ref)

    _ssd_chunk(0, xf_ref, btf_ref, cf_ref, bf_ref, dtf_ref, dttf_ref, dthf_ref, a_ref, at_ref, ah_ref,
               yf_ref, st_ref)
    _ssd_chunk(1, xb_ref, btb_ref, cb_ref, bb_ref, dtb_ref, dttb_ref, dthb_ref, a_ref, at_ref, ah_ref,
               yb_ref, st_ref)


def ssd_scan(xs, bm, cm, dt, dth, a, ctx_len):
    bsz, t, _ = xs.shape
    q = SSM_CHUNK
    nc = t // q
    nctx = ctx_len // q
    bt = jnp.swapaxes(bm, 1, 2)
    dtt = jnp.swapaxes(dt, 2, 3)

    def cidx(d, j):
        return j if d == 0 else jnp.where(j < nctx, nctx - 1 - j, nc - 1 - (j - nctx))

    def specs(d):
        return [pl.BlockSpec((1, q, SSM_W), lambda b, j: (b, cidx(d, j), 0)),
                pl.BlockSpec((1, SSM_GN, q), lambda b, j: (b, 0, cidx(d, j))),
                pl.BlockSpec((1, q, SSM_GN), lambda b, j: (b, cidx(d, j), 0)),
                pl.BlockSpec((1, q, SSM_GN), lambda b, j: (b, cidx(d, j), 0)),
                pl.BlockSpec((1, 1, q, SSM_HEADS), lambda b, j: (b, d, cidx(d, j), 0)),
                pl.BlockSpec((1, 1, SSM_HEADS, q), lambda b, j: (b, d, 0, cidx(d, j))),
                pl.BlockSpec((1, q, SSM_W), lambda b, j: (b, cidx(d, j), d))]

    dir_args = (xs, bt, cm, bm, dt, dtt, dth)
    whole = lambda shape: pl.BlockSpec(shape, lambda b, j: (0, 0, 0))
    return pl.pallas_call(
        _ssd_kernel,
        grid=(bsz, nc),
        in_specs=specs(0) + specs(1) + [whole((2, 1, SSM_HEADS)), whole((2, SSM_HEADS, 1)), whole((2, 1, SSM_W))],
        out_specs=[pl.BlockSpec((1, q, SSM_W), lambda b, j: (b, cidx(0, j), 0)),
                   pl.BlockSpec((1, q, SSM_W), lambda b, j: (b, cidx(1, j), 0))],
        out_shape=[jax.ShapeDtypeStruct((bsz, t, SSM_W), F32)] * 2,
        scratch_shapes=[pltpu.VMEM((2, SSM_GROUPS, SSM_STATE, SSM_HPG * HEAD_DIM), F32)],
        compiler_params=_cparams("parallel", "arbitrary"),
        name="ssd_scan",
    )(*dir_args, *dir_args, a.reshape(2, 1, SSM_HEADS), a.reshape(2, SSM_HEADS, 1),
      jnp.repeat(a, HEAD_DIM, axis=1).reshape(2, 1, SSM_W))


def _ssm_prep_kernel(u_ref, up_ref, un_ref, cw_ref, cb_ref, dtb_ref, spread_ref, xs_out, b_out, c_out, dt_out,
                     dth_out, *, blocks_per_batch, ctx_blocks):
    first, last = _segment_edges(pl.program_id(0), blocks_per_batch, ctx_blocks)
    lo, hi = SSM_W, SSM_W + SSM_XBC
    xe = _with_halo(u_ref[:, lo:hi], up_ref[:, lo:hi], un_ref[:, lo:hi], first, last)
    acc = cb_ref[...] + cw_ref[0:1, :] * _shifted(xe, SSM_CONV // 2)
    for i in range(1, SSM_CONV):
        acc = acc + cw_ref[i:i + 1, :] * _shifted(xe, SSM_CONV // 2 - i)
    xbc = acc * jax.nn.sigmoid(acc)
    xs_out[...] = xbc[:, :SSM_W]
    b_out[...] = xbc[:, SSM_W:SSM_W + SSM_GN].astype(b_out.dtype)
    c_out[...] = xbc[:, SSM_W + SSM_GN:].astype(c_out.dtype)
    dt = _softplus(u_ref[:, hi:] + dtb_ref[...])
    dt_out[...] = dt
    dth_out[...] = _dot_f32_mask(dt, spread_ref[...])


def ssm_prep(uc, blocks_per_batch, ctx_blocks, conv_w, conv_b, dt_bias):
    nt = uc.shape[0]
    npad = SSM_PAD - SSM_W - SSM_XBC
    dtb = jnp.pad(dt_bias.reshape(-1), (0, npad - 2 * SSM_HEADS)).reshape(1, npad)
    spread = np.zeros((npad, 2 * SSM_W), np.float32)
    spread[:2 * SSM_HEADS] = np.repeat(np.eye(2 * SSM_HEADS, dtype=np.float32), HEAD_DIM, axis=1)
    const = lambda shape: pl.BlockSpec(shape, lambda i: (0, 0))
    row = lambda n: pl.BlockSpec((TM, n), lambda i: (i, 0))
    return pl.pallas_call(
        functools.partial(_ssm_prep_kernel, blocks_per_batch=blocks_per_batch, ctx_blocks=ctx_blocks),
        grid=(nt // TM,),
        in_specs=_halo_specs(nt, SSM_PAD) + [const((SSM_CONV, SSM_XBC)), const((1, SSM_XBC)), const((1, npad)),
                                             const((npad, 2 * SSM_W))],
        out_specs=[row(SSM_W), row(SSM_GN), row(SSM_GN), row(npad), row(2 * SSM_W)],
        out_shape=[jax.ShapeDtypeStruct((nt, SSM_W), F32), jax.ShapeDtypeStruct((nt, SSM_GN), BF16),
                   jax.ShapeDtypeStruct((nt, SSM_GN), BF16), jax.ShapeDtypeStruct((nt, npad), F32),
                   jax.ShapeDtypeStruct((nt, 2 * SSM_W), F32)],
        compiler_params=_cparams("parallel"),
        name="ssm_prep",
    )(uc, uc, uc, conv_w, conv_b.reshape(1, SSM_XBC), dtb, jnp.asarray(spread, BF16))


def _ssm_post_kernel(yf_ref, yb_ref, xs_ref, z_ref, d_ref, nw_ref, o_ref):
    z = z_ref[...]
    y = (yf_ref[...] + yb_ref[...] + d_ref[...] * xs_ref[...]) * (z * jax.nn.sigmoid(z))
    gw = SSM_HPG * HEAD_DIM
    for g in range(SSM_GROUPS):
        gs = slice(g * gw, (g + 1) * gw)
        yg = y[:, gs]
        ms = jnp.mean(yg * yg, -1, keepdims=True)
        o_ref[:, gs] = (yg * lax.rsqrt(ms + RMS_EPS) * nw_ref[:, gs]).astype(o_ref.dtype)


def ssm_post(yf, yb, xs, uc, d_skip, norm_w):
    nt = xs.shape[0]
    row = pl.BlockSpec((TM, SSM_W), lambda i: (i, 0))
    const = pl.BlockSpec((1, SSM_W), lambda i: (0, 0))
    return pl.pallas_call(
        _ssm_post_kernel,
        grid=(nt // TM,),
        in_specs=[row, row, row, row, const, const],
        out_specs=row,
        out_shape=jax.ShapeDtypeStruct((nt, SSM_W), BF16),
        compiler_params=_cparams("parallel"),
        name="ssm_post",
    )(yf, yb, xs, uc, jnp.repeat(d_skip, HEAD_DIM).reshape(1, SSM_W), norm_w.reshape(1, SSM_W))


def ssm_mixer(uc, bsz, ctx_len, conv_w, conv_b, dt_bias, a_log, d_skip, norm_w):
    nt = uc.shape[0]
    tp = nt // bsz
    xs, bm, cm, dtp, dth = ssm_prep(uc, tp // TM, ctx_len // TM, conv_w, conv_b, dt_bias)
    dt = jnp.transpose(dtp[:, :2 * SSM_HEADS].reshape(bsz, tp, 2, SSM_HEADS), (0, 2, 1, 3))
    a = -jnp.exp(a_log.astype(F32)).reshape(2, SSM_HEADS)
    yf, yb = ssd_scan(xs.reshape(bsz, tp, SSM_W), bm.reshape(bsz, tp, SSM_GN), cm.reshape(bsz, tp, SSM_GN),
                      dt, dth.reshape(bsz, tp, 2 * SSM_W), a, ctx_len)
    return ssm_post(yf.reshape(nt, SSM_W), yb.reshape(nt, SSM_W), xs, uc, d_skip, norm_w)


def _adaln_router_kernel(x_ref, sh_ref, sc_ref, wr_ref, f_ref, lg_ref):
    xm = x_ref[...] * (1.0 + sc_ref[0]) + sh_ref[0]
    f_ref[...] = xm.astype(f_ref.dtype)
    lg_ref[...] = jnp.dot(xm, wr_ref[...], precision=lax.Precision.HIGHEST, preferred_element_type=F32)


def adaln_router(x, mod_tab, sh_idx, sc_idx, wr):
    nt, d = x.shape
    n = wr.shape[1]
    return pl.pallas_call(
        _adaln_router_kernel,
        grid=(nt // TM,),
        in_specs=[pl.BlockSpec((TM, d), lambda i: (i, 0))] + _mod_specs(d, sh_idx, sc_idx)
        + [pl.BlockSpec((d, n), lambda i: (0, 0))],
        out_specs=[pl.BlockSpec((TM, d), lambda i: (i, 0)), pl.BlockSpec((TM, n), lambda i: (i, 0))],
        out_shape=[jax.ShapeDtypeStruct((nt, d), BF16), jax.ShapeDtypeStruct((nt, n), F32)],
        compiler_params=_cparams("parallel"),
        name="adaln_router",
    )(x, mod_tab, mod_tab, wr)


def _expert_changed(be_ref, i):
    return jnp.logical_or(i == 0, be_ref[i] != be_ref[jnp.maximum(i - 1, 0)])


def _moe_up_kernel(be_ref, x_ref, wg_ref, wu_ref, o_ref, wg_bf, wu_bf):
    @pl.when(_expert_changed(be_ref, pl.program_id(1)))
    def _cast():
        wg_bf[...] = wg_ref[0, 0].astype(BF16)
        wu_bf[...] = wu_ref[0, 0].astype(BF16)

    x = x_ref[...]
    g = _dot(x, wg_bf[...])
    u = _dot(x, wu_bf[...])
    o_ref[...] = (g * jax.nn.sigmoid(g) * u).astype(o_ref.dtype)


def moe_up(xb, block_e, wg, wu, layer, n_split=2):
    ns, d = xb.shape
    f = wg.shape[3]
    tn = f // n_split
    return pl.pallas_call(
        _moe_up_kernel,
        grid_spec=pltpu.PrefetchScalarGridSpec(
            num_scalar_prefetch=1,
            grid=(n_split, ns // MOE_BLOCK),
            in_specs=[pl.BlockSpec((MOE_BLOCK, d), lambda n, i, be: (i, 0)),
                      pl.BlockSpec((1, 1, d, tn), lambda n, i, be: (layer, be[i], 0, n)),
                      pl.BlockSpec((1, 1, d, tn), lambda n, i, be: (layer, be[i], 0, n))],
            out_specs=pl.BlockSpec((MOE_BLOCK, tn), lambda n, i, be: (i, n)),
            scratch_shapes=[pltpu.VMEM((d, tn), BF16), pltpu.VMEM((d, tn), BF16)]),
        out_shape=jax.ShapeDtypeStruct((ns, f), BF16),
        compiler_params=_cparams("arbitrary", "arbitrary"),
        name="moe_up",
    )(block_e, xb, wg, wu)


def _moe_down_kernel(be_ref, a_ref, wd_ref, o_ref, wd_bf):
    @pl.when(_expert_changed(be_ref, pl.program_id(0)))
    def _cast():
        wd_bf[...] = wd_ref[0, 0].astype(BF16)

    o_ref[...] = _dot(a_ref[...], wd_bf[...])


def moe_down(act, block_e, wd, layer):
    ns, f = act.shape
    d = wd.shape[3]
    return pl.pallas_call(
        _moe_down_kernel,
        grid_spec=pltpu.PrefetchScalarGridSpec(
            num_scalar_prefetch=1,
            grid=(ns // MOE_BLOCK,),
            in_specs=[pl.BlockSpec((MOE_BLOCK, f), lambda i, be: (i, 0)),
                      pl.BlockSpec((1, 1, f, d), lambda i, be: (layer, be[i], 0, 0))],
            out_specs=pl.BlockSpec((MOE_BLOCK, d), lambda i, be: (i, 0)),
            scratch_shapes=[pltpu.VMEM((f, d), BF16)]),
        out_shape=jax.ShapeDtypeStruct((ns, d), F32),
        compiler_params=_cparams("arbitrary"),
        name="moe_down",
    )(block_e, act, wd)


def _moe_res_ln_kernel(h_ref, y0_ref, y1_ref, gt_ref, gate_ref, g_ref, b_ref, out_ref):
    gt = gt_ref[...]
    y = y0_ref[...] * gt[:, 0:1] + y1_ref[...] * gt[:, 1:2]
    z = DEEPNORM_ALPHA * h_ref[...] + gate_ref[0] * y
    out_ref[...] = _layer_norm(z, g_ref[...], b_ref[...])


def moe_res_ln(h, y0, y1, gates, mod_tab, gate_idx, ln_g, ln_b):
    nt, d = h.shape
    row = pl.BlockSpec((TM, d), lambda i: (i, 0))
    return pl.pallas_call(
        _moe_res_ln_kernel,
        grid=(nt // TM,),
        in_specs=[row, row, row,
                  pl.BlockSpec((TM, TOP_K), lambda i: (i, 0)),
                  pl.BlockSpec((1, 1, d), lambda i: (i, 0, gate_idx)),
                  pl.BlockSpec((1, d), lambda i: (0, 0)),
                  pl.BlockSpec((1, d), lambda i: (0, 0))],
        out_specs=row,
        out_shape=jax.ShapeDtypeStruct((nt, d), F32),
        compiler_params=_cparams("parallel"),
        name="moe_res_ln",
    )(h, y0, y1, gates, mod_tab, ln_g.reshape(1, d), ln_b.reshape(1, d))


def moe_mixer(h, mod_tab, router_w, router_b, wg, wu, wd, layer, ln_g, ln_b):
    n_tok, d = h.shape
    wr = jnp.pad(router_w, ((0, 0), (0, ROUTER_PAD - N_EXPERTS)))
    f_bf, logits = adaln_router(h, mod_tab, 3, 4, wr)
    logits = logits[:, :N_EXPERTS] + router_b
    top_val, top_idx = lax.top_k(logits, TOP_K)
    gates = jax.nn.softmax(top_val, axis=-1)
    n_assign = n_tok * TOP_K
    flat_e = top_idx.reshape(-1)
    onehot = (flat_e[:, None] == jnp.arange(N_EXPERTS)[None, :]).astype(jnp.int32)
    csum = jnp.cumsum(onehot, axis=0)
    counts = csum[-1]
    rank = jnp.sum((csum - onehot) * onehot, axis=1)
    padded = (counts + MOE_BLOCK - 1) // MOE_BLOCK * MOE_BLOCK
    pad_end = jnp.cumsum(padded)
    pad_start = pad_end - padded
    dest = pad_start[flat_e] + rank
    n_blocks = -(-n_assign // MOE_BLOCK) + N_EXPERTS
    flat_tok = jnp.repeat(jnp.arange(n_tok, dtype=jnp.int32), TOP_K)
    slot_tok = jnp.full((n_blocks * MOE_BLOCK,), n_tok, jnp.int32).at[dest].set(flat_tok)
    block_e = jnp.minimum(jnp.searchsorted(pad_end, jnp.arange(n_blocks) * MOE_BLOCK, side='right'),
                          N_EXPERTS - 1).astype(jnp.int32)
    f_pad = jnp.concatenate([f_bf, jnp.zeros((1, d), f_bf.dtype)], axis=0)
    xb = f_pad[slot_tok]
    act = moe_up(xb, block_e, wg, wu, layer)
    yb = moe_down(act, block_e, wd, layer)
    dest2 = dest.reshape(n_tok, TOP_K)
    return moe_res_ln(h, yb[dest2[:, 0]], yb[dest2[:, 1]], gates, mod_tab, 5, ln_g, ln_b)


def kernel(x, c, ctx, c_ctx, w_mod, b_mod, w_in, w_out, ln1_g, ln1_b, ln2_g, ln2_b, na_rpb, rw_mu, rw_w0, rw_w2, rw_a0, rw_a2, rw_g2, rw_k_k, rw_k_a, rw_r_k, rw_ln_w, rw_ln_b, ssm_conv_w, ssm_conv_b, ssm_dt_bias, ssm_a_log, ssm_d, ssm_norm_w, ffn_w_gate, ffn_w_up, ffn_w_down, moe_router, moe_router_b, moe_w_gate, moe_w_up, moe_w_down):
    bsz, seq, d = x.shape
    ctx_len = ctx.shape[1]
    tp = ctx_len + seq
    nt = bsz * tp
    assert ctx_len % TM == 0 and seq % TM == 0 and RW_KQ * bsz * RW_HEADS == LANES
    rows = seq // GRID_W
    depth = w_mod.shape[0]

    h = jnp.concatenate([ctx, x], axis=1).reshape(nt, d)

    n_mod_rows = -(-(bsz + 1) // 16) * 16
    cin = jnp.concatenate([c, c_ctx[None, :], jnp.zeros((n_mod_rows - bsz - 1, d), F32)], axis=0)
    mod_all = modulation(cin, w_mod, b_mod)
    blk = np.arange(nt // TM)
    blk_row = np.where(blk % (tp // TM) < ctx_len // TM, bsz, blk // (tp // TM))

    for l in range(depth):
        mod_tab = mod_all[l][blk_row][:, None, :]
        wl = w_in[l]
        w_in_p = jnp.concatenate(
            [wl[:, :NA_COLS],
             jnp.pad(wl[:, NA_COLS:NA_COLS + RW_COLS], ((0, 0), (0, RW_PAD - RW_COLS))),
             jnp.pad(wl[:, NA_COLS + RW_COLS:], ((0, 0), (0, SSM_PAD - SSM_COLS)))], axis=1).astype(BF16)
        ua, ub, uc = adaln_proj(h, mod_tab, 0, 1, w_in_p, (NA_COLS, RW_PAD, SSM_PAD), (BF16, F32, F32))
        rw_args, rw_gb = rwkv_front(ub, bsz, ctx_len, rw_mu[l], rw_w0[l], rw_w2[l], rw_a0[l], rw_a2[l],
                                    rw_g2[l], rw_k_k[l], rw_k_a[l], rw_r_k[l])
        oa = na_attention(ua.reshape(bsz, tp, NA_COLS), _na_bias_table(na_rpb[l], rows), ctx_len)
        oc = ssm_mixer(uc, bsz, ctx_len, ssm_conv_w[l], ssm_conv_b[l], ssm_dt_bias[l], ssm_a_log[l],
                       ssm_d[l], ssm_norm_w[l])
        ob = rwkv_back(rw_args, rw_gb, bsz, ctx_len, rw_ln_w[l], rw_ln_b[l])
        h = mix_proj_res_ln(oa.reshape(nt, NA_W), ob, oc, w_out[l].astype(BF16), h, mod_tab, 2,
                            ln1_g[l], ln1_b[l])

        j = l // 2
        if l % 2 == 0:
            act = adaln_swiglu(h, mod_tab, 3, 4, ffn_w_gate[j].astype(BF16), ffn_w_up[j].astype(BF16))
            h = proj_res_ln(act, ffn_w_down[j].astype(BF16), h, mod_tab, 5, ln2_g[l], ln2_b[l])
        else:
            h = moe_mixer(h, mod_tab, moe_router[j], moe_router_b[j], moe_w_gate, moe_w_up, moe_w_down, j,
                          ln2_g[l], ln2_b[l])
    return h.reshape(bsz, tp, d)[:, ctx_len:]
```
